```python
import math
import jax
import jax.numpy as jnp
from jax import lax
import numpy as np

D_MODEL = 4096
BATCH = 1
SEQ = 8192
DEPTH = 2

CTX_LEN = 256
GRID_W = 64
EPS = 1e-6

MLA_HEADS = D_MODEL // 256
MLA_NOPE = 128
MLA_ROPE = 64
MLA_V = 128
MLA_Q_RANK = D_MODEL // 4
MLA_KV_RANK = 512
ROPE_THETA = 10000.0
Q_BLOCK = 128

GDN_HEADS = D_MODEL // 512
GDN_DK = 128
GDN_DV = 128
GDN_CHUNK = 64
GDN_CONV = 4

RG_WIDTH = D_MODEL // 4
RG_BLOCKS = 8
RG_C = 8.0
RG_CONV = 4

MLA_WIDTH = MLA_HEADS * MLA_V
GDN_WIDTH = GDN_HEADS * GDN_DV
D_MIX = MLA_WIDTH + GDN_WIDTH + RG_WIDTH
GDN_QKV = GDN_HEADS * (2 * GDN_DK + GDN_DV)

IN_SIZES = (MLA_Q_RANK, MLA_KV_RANK, MLA_ROPE, GDN_QKV, GDN_WIDTH, 2 * GDN_HEADS, 2 * GDN_HEADS, RG_WIDTH, RG_WIDTH)
D_IN = MLA_Q_RANK + MLA_KV_RANK + MLA_ROPE + GDN_QKV + GDN_WIDTH + 4 * GDN_HEADS + 2 * RG_WIDTH

N_EXPERTS = 16
EXPERT_FF = D_MODEL // 4
CAPACITY_FACTOR = 2

kernel_name = 'hybrid_mla_gdn_rglru_ecmoe_dit'


def rmsnorm(x, g):
    xf = x.astype(jnp.float32)
    y = xf * lax.rsqrt(jnp.mean(xf * xf, axis=-1, keepdims=True) + EPS)
    return (y * g.astype(jnp.float32)).astype(x.dtype)


def l2norm(t):
    tf = t.astype(jnp.float32)
    return tf * lax.rsqrt(jnp.sum(tf * tf, axis=-1, keepdims=True) + EPS)


def split_columns(z, sizes):
    parts, start = [], 0
    for s in sizes:
        parts.append(z[..., start:start + s])
        start += s
    return parts


def flip_seq(t):
    return None if t is None else jnp.flip(t, axis=1)


def centred_conv(x, w):
    width = w.shape[0]
    left = width // 2
    return lax.conv_general_dilated(
        x, w[:, None, :].astype(x.dtype), window_strides=(1,),
        padding=[(left, width - 1 - left)],
        dimension_numbers=('NWC', 'WIO', 'NWC'),
        feature_group_count=x.shape[-1])


def axial_rope(seq):
    rows = seq // GRID_W
    row = jnp.broadcast_to(jnp.arange(rows, dtype=jnp.float32)[:, None], (rows, GRID_W)).reshape(-1)
    col = jnp.broadcast_to(jnp.arange(GRID_W, dtype=jnp.float32)[None, :], (rows, GRID_W)).reshape(-1)
    n_freq = MLA_ROPE // 4
    inv_freq = ROPE_THETA ** (-jnp.arange(n_freq, dtype=jnp.float32) / n_freq)
    ang = jnp.concatenate([row[:, None] * inv_freq, col[:, None] * inv_freq], axis=-1)
    return jnp.cos(ang), jnp.sin(ang)


def apply_rope(t, cos, sin):
    half = t.shape[-1] // 2
    t1 = t[..., :half].astype(jnp.float32)
    t2 = t[..., half:].astype(jnp.float32)
    return jnp.concatenate([t1 * cos - t2 * sin, t2 * cos + t1 * sin], axis=-1).astype(t.dtype)


def mla_mixer(zq, zkv, zpe, zq_c, zkv_c, zpe_c, q_norm, kv_norm, w_qb, w_kvb, cos, sin, need_ctx):
    B, S, _ = zq.shape
    H = MLA_HEADS
    scale = (MLA_NOPE + MLA_ROPE) ** -0.5

    def queries(z):
        q = (rmsnorm(z, q_norm) @ w_qb).reshape(z.shape[0], z.shape[1], H, MLA_NOPE + MLA_ROPE)
        return q[..., :MLA_NOPE], q[..., MLA_NOPE:]

    def keys_values(z):
        kv = (rmsnorm(z, kv_norm) @ w_kvb).reshape(z.shape[0], z.shape[1], H, MLA_NOPE + MLA_V)
        return kv[..., :MLA_NOPE], kv[..., MLA_NOPE:]

    k_nope, v = keys_values(zkv)
    k_rot = apply_rope(zpe, cos, sin)
    kc_nope, vc = keys_values(zkv_c)
    n_ctx = zkv_c.shape[1]
    q_nope, q_pe = queries(zq)
    q_rot = apply_rope(q_pe, cos[:, None], sin[:, None])

    def attend_block(blk):
        qn, qr, qp = blk
        s_ctx = jnp.einsum('bqhd,bkhd->bhqk', qn, kc_nope) + jnp.einsum('bqhr,bkr->bhqk', qp, zpe_c)
        s_lat = jnp.einsum('bqhd,bkhd->bhqk', qn, k_nope) + jnp.einsum('bqhr,bkr->bhqk', qr, k_rot)
        s = jnp.concatenate([s_ctx, s_lat], axis=-1).astype(jnp.float32) * scale
        p = jax.nn.softmax(s, axis=-1).astype(v.dtype)
        return (jnp.einsum('bhqk,bkhd->bqhd', p[..., :n_ctx], vc)
                + jnp.einsum('bhqk,bkhd->bqhd', p[..., n_ctx:], v))

    nb = S // Q_BLOCK

    def to_blocks(t):
        return jnp.swapaxes(t.reshape(B, nb, Q_BLOCK, *t.shape[2:]), 0, 1)

    o = lax.map(attend_block, (to_blocks(q_nope), to_blocks(q_rot), to_blocks(q_pe)))
    o = jnp.swapaxes(o, 0, 1).reshape(B, S, H * MLA_V)
    o_c = None
    if need_ctx:
        qc_nope, qc_pe = queries(zq_c)
        s = (jnp.einsum('bqhd,bkhd->bhqk', qc_nope, kc_nope)
             + jnp.einsum('bqhr,bkr->bhqk', qc_pe, zpe_c)).astype(jnp.float32) * scale
        p = jax.nn.softmax(s, axis=-1).astype(vc.dtype)
        o_c = jnp.einsum('bhqk,bkhd->bqhd', p, vc).reshape(B, n_ctx, H * MLA_V)
    return o, o_c


def gdn_prepare(zqkv, zb, za, conv_w, a_log, dt_bias, with_q):
    B, T, _ = zqkv.shape
    nq = GDN_HEADS * GDN_DK
    kv = jax.nn.silu(centred_conv(zqkv[..., nq:], conv_w[:, nq:]))
    k = l2norm(kv[..., :nq].reshape(B, T, GDN_HEADS, GDN_DK))
    v = kv[..., nq:].reshape(B, T, GDN_HEADS, GDN_DV).astype(jnp.float32)
    beta = jax.nn.sigmoid(zb.astype(jnp.float32)).reshape(B, T, 2, GDN_HEADS)
    g = -jnp.exp(a_log.astype(jnp.float32)) * jax.nn.softplus(
        za.astype(jnp.float32).reshape(B, T, 2, GDN_HEADS) + dt_bias.astype(jnp.float32))
    q = None
    if with_q:
        q = l2norm(jax.nn.silu(centred_conv(zqkv[..., :nq], conv_w[:, :nq])).reshape(B, T, GDN_HEADS, GDN_DK))
        q = q * (GDN_DK ** -0.5)
    return q, k, v, g, beta


def gated_delta_chunked(k, v, g, beta, s0, q):
    B, T, H, _ = k.shape
    C = GDN_CHUNK
    n = T // C

    def chunks(t):
        return jnp.moveaxis(t.reshape(B, n, C, H, *t.shape[3:]), (1, 3), (0, 2))

    kf, vf, bc = chunks(k), chunks(v), chunks(beta)
    gc = jnp.cumsum(chunks(g), axis=-1)
    idx = jnp.arange(C)
    causal = idx[:, None] >= idx[None, :]
    strict = idx[:, None] > idx[None, :]
    decay = jnp.exp(jnp.where(causal, gc[..., :, None] - gc[..., None, :], -jnp.inf))
    a_low = jnp.where(strict, jnp.einsum('nbhik,nbhjk->nbhij', kf, kf) * bc[..., :, None] * decay, 0.0)
    lhs = jnp.eye(C, dtype=jnp.float32) + a_low
    u = lax.linalg.triangular_solve(lhs, vf * bc[..., None], left_side=True, lower=True, unit_diagonal=True)
    w = lax.linalg.triangular_solve(lhs, kf * (bc * jnp.exp(gc))[..., None], left_side=True, lower=True,
                                    unit_diagonal=True)
    g_last = gc[..., -1]
    k_tail = kf * jnp.exp(g_last[..., None] - gc)[..., None]
    with_q = q is not None
    xs = (u, w, k_tail, g_last)
    if with_q:
        qf = chunks(q)
        xs = xs + (qf * jnp.exp(gc)[..., None], jnp.einsum('nbhik,nbhjk->nbhij', qf, kf) * decay)

    def step(S, inp):
        u_c, w_c, kt_c, gl_c = inp[:4]
        v_new = u_c - jnp.einsum('bhck,bhkv->bhcv', w_c, S)
        S_next = S * jnp.exp(gl_c)[..., None, None] + jnp.einsum('bhck,bhcv->bhkv', kt_c, v_new)
        if with_q:
            qd_c, qk_c = inp[4], inp[5]
            o = jnp.einsum('bhck,bhkv->bhcv', qd_c, S) + jnp.einsum('bhij,bhjv->bhiv', qk_c, v_new)
            return S_next, o
        return S_next, None

    s_fin, o = lax.scan(step, s0, xs)
    if with_q:
        o = jnp.moveaxis(o, (0, 2), (1, 3)).reshape(B, T, H, GDN_DV)
    return o, s_fin


def gdn_direction(q, k, v, g, beta, s0, reverse):
    if reverse:
        q, k, v, g, beta = (flip_seq(t) for t in (q, k, v, g, beta))
    o, s = gated_delta_chunked(k, v, g, beta, s0, q)
    return (flip_seq(o) if reverse else o), s


def gated_rmsnorm(o, zg, w):
    B, T = zg.shape[:2]
    on = o * lax.rsqrt(jnp.mean(o * o, axis=-1, keepdims=True) + EPS) * w.astype(jnp.float32)
    return (on.reshape(B, T, -1) * jax.nn.silu(zg.astype(jnp.float32))).astype(zg.dtype)


def gdn_mixer(zqkv, zg, zb, za, zqkv_c, zg_c, zb_c, za_c, conv_w, a_log, dt_bias, o_norm, need_ctx):
    q, k, v, g, beta = gdn_prepare(zqkv, zb, za, conv_w, a_log, dt_bias, True)
    qc, kc, vc, gc, bc = gdn_prepare(zqkv_c, zb_c, za_c, conv_w, a_log, dt_bias, need_ctx)
    s0 = jnp.zeros((k.shape[0], GDN_HEADS, GDN_DK, GDN_DV), jnp.float32)
    o_dirs, oc_dirs = [], []
    for d, reverse in ((0, False), (1, True)):
        oc_d, s_ctx = gdn_direction(qc, kc, vc, gc[:, :, d], bc[:, :, d], s0, reverse)
        o_d, _ = gdn_direction(q, k, v, g[:, :, d], beta[:, :, d], s_ctx, reverse)
        o_dirs.append(o_d)
        oc_dirs.append(oc_d)
    out = gated_rmsnorm(o_dirs[0] + o_dirs[1], zg, o_norm)
    out_c = gated_rmsnorm(oc_dirs[0] + oc_dirs[1], zg_c, o_norm) if need_ctx else None
    return out, out_c


def linear_recurrence(a, b, h0):
    def combine(l, r):
        return l[0] * r[0], r[0] * l[1] + r[1]
    a_cum, h = lax.associative_scan(combine, (a, b), axis=1)
    h = h + a_cum * h0[:, None, :]
    return h, h[:, -1]


def rglru_direction(x, w_a, b_a, w_x, b_x, lam, h0, reverse):
    if reverse:
        x = flip_seq(x)
    B, T, R = x.shape
    xb = x.reshape(B, T, RG_BLOCKS, R // RG_BLOCKS)
    r = jax.nn.sigmoid(jnp.einsum('btni,nij->btnj', xb, w_a.astype(jnp.float32)).reshape(B, T, R)
                       + b_a.astype(jnp.float32))
    i = jax.nn.sigmoid(jnp.einsum('btni,nij->btnj', xb, w_x.astype(jnp.float32)).reshape(B, T, R)
                       + b_x.astype(jnp.float32))
    log_a = -RG_C * r * jax.nn.softplus(-lam.astype(jnp.float32))
    u = jnp.sqrt(-jnp.expm1(2.0 * log_a)) * (i * x)
    h, h_last = linear_recurrence(jnp.exp(log_a), u, h0)
    return (flip_seq(h) if reverse else h), h_last


def rglru_mixer(zx, zy, zx_c, zy_c, conv_w, conv_b, w_a, b_a, w_x, b_x, lam, need_ctx):
    x = (centred_conv(zx, conv_w) + conv_b.astype(zx.dtype)).astype(jnp.float32)
    xc = (centred_conv(zx_c, conv_w) + conv_b.astype(zx_c.dtype)).astype(jnp.float32)
    h0 = jnp.zeros((x.shape[0], RG_WIDTH), jnp.float32)
    h_dirs, hc_dirs = [], []
    for d, reverse in ((0, False), (1, True)):
        hc, s_ctx = rglru_direction(xc, w_a[d], b_a[d], w_x[d], b_x[d], lam[d], h0, reverse)
        h, _ = rglru_direction(x, w_a[d], b_a[d], w_x[d], b_x[d], lam[d], s_ctx, reverse)
        h_dirs.append(h)
        hc_dirs.append(hc)
    out = ((h_dirs[0] + h_dirs[1]) * jax.nn.gelu(zy.astype(jnp.float32))).astype(zx.dtype)
    out_c = None
    if need_ctx:
        out_c = ((hc_dirs[0] + hc_dirs[1]) * jax.nn.gelu(zy_c.astype(jnp.float32))).astype(zx_c.dtype)
    return out, out_c


def expert_choice_ffn(h, w_router, w_gate, w_up, w_down):
    B, N, D = h.shape
    cap = CAPACITY_FACTOR * N // N_EXPERTS
    aff = jax.nn.softmax((h @ w_router).astype(jnp.float32), axis=-1)
    gates, idx = lax.top_k(jnp.swapaxes(aff, 1, 2), cap)
    xg = jax.vmap(lambda hb, ib: hb[ib])(h, idx)
    hid = jax.nn.silu(jnp.einsum('becd,edf->becf', xg, w_gate)) * jnp.einsum('becd,edf->becf', xg, w_up)
    y = jnp.einsum('becf,efd->becd', hid, w_down) * gates[..., None].astype(h.dtype)
    return jax.vmap(lambda yb, ib: jnp.zeros((N, D), h.dtype).at[ib.reshape(-1)].add(yb.reshape(-1, D)))(y, idx)


def hybrid_layer(x, xc, c, c_ctx, p, update_ctx):
    seq = x.shape[1]
    mod = (jax.nn.silu(c) @ p['mod_w'] + p['mod_b'])[:, None, :]
    mod_c = (jax.nn.silu(c_ctx) @ p['mod_w'] + p['mod_b'])[None, None, :]
    sh_m, sc_m, g_m, sh_f, sc_f, g_f = jnp.split(mod, 6, axis=-1)
    shc_m, scc_m, gc_m, shc_f, scc_f, gc_f = jnp.split(mod_c, 6, axis=-1)

    h = rmsnorm(x, p['norm_mix_pre']) * (1.0 + sc_m) + sh_m
    hc = rmsnorm(xc, p['norm_mix_pre']) * (1.0 + scc_m) + shc_m
    q_a, kv_a, k_pe, g_qkv, g_z, g_b, g_a, r_x, r_y = split_columns(h @ p['w_in'], IN_SIZES)
    qc_a, kvc_a, kc_pe, gc_qkv, gc_z, gc_b, gc_a, rc_x, rc_y = split_columns(hc @ p['w_in'], IN_SIZES)

    cos, sin = axial_rope(seq)
    o_mla, oc_mla = mla_mixer(q_a, kv_a, k_pe, qc_a, kvc_a, kc_pe, p['mla_q_norm'], p['mla_kv_norm'],
                              p['mla_w_qb'], p['mla_w_kvb'], cos, sin, update_ctx)
    o_gdn, oc_gdn = gdn_mixer(g_qkv, g_z, g_b, g_a, gc_qkv, gc_z, gc_b, gc_a, p['gdn_conv_w'], p['gdn_a_log'],
                              p['gdn_dt_bias'], p['gdn_o_norm'], update_ctx)
    o_rg, oc_rg = rglru_mixer(r_x, r_y, rc_x, rc_y, p['rg_conv_w'], p['rg_conv_b'], p['rg_w_a'], p['rg_b_a'],
                              p['rg_w_x'], p['rg_b_x'], p['rg_lambda'], update_ctx)

    y = jnp.concatenate([o_mla, o_gdn, o_rg], axis=-1) @ p['w_out']
    x = x + g_m * rmsnorm(y, p['norm_mix_post'])
    h = rmsnorm(x, p['norm_ffn_pre']) * (1.0 + sc_f) + sh_f
    y = expert_choice_ffn(h, p['w_router'], p['w_gate'], p['w_up'], p['w_down'])
    x = x + g_f * rmsnorm(y, p['norm_ffn_post'])

    if update_ctx:
        yc = jnp.concatenate([oc_mla, oc_gdn, oc_rg], axis=-1) @ p['w_out']
        xc = xc + gc_m * rmsnorm(yc, p['norm_mix_post'])
        hc = rmsnorm(xc, p['norm_ffn_pre']) * (1.0 + scc_f) + shc_f
        yc = expert_choice_ffn(hc, p['w_router'], p['w_gate'], p['w_up'], p['w_down'])
        xc = xc + gc_f * rmsnorm(yc, p['norm_ffn_post'])
    return x, xc


def setup_inputs(seed: int = 0) -> dict:
    key = jax.random.key(seed)
    ks = jax.random.split(key, 32)
    L = DEPTH
    f32 = jnp.float32
    D = D_MODEL
    bs = RG_WIDTH // RG_BLOCKS

    def nrm(k, shape, scale):
        return jax.random.normal(k, shape, f32) * scale

    def gain(k, shape):
        return 1.0 + 0.02 * jax.random.normal(k, shape, f32)

    dt = jnp.exp(jax.random.uniform(ks[16], (L, 2, GDN_HEADS), f32, math.log(1e-3), math.log(1e-1)))
    a0 = jax.random.uniform(ks[25], (L, 2, RG_WIDTH), f32, 0.9, 0.999) ** (1.0 / RG_C)
    return {
        'x': nrm(ks[0], (BATCH, SEQ, D), 1.0),
        'c': nrm(ks[1], (BATCH, D), 1.0),
        'ctx': nrm(ks[2], (BATCH, CTX_LEN, D), 1.0),
        'c_ctx': nrm(ks[3], (D,), 1.0),
        'mod_w': nrm(ks[4], (L, D, 6 * D), 0.5 * D ** -0.5),
        'mod_b': nrm(ks[5], (L, 6 * D), 0.01),
        'norm_mix_pre': gain(ks[6], (L, D)),
        'norm_mix_post': gain(ks[7], (L, D)),
        'norm_ffn_pre': gain(ks[8], (L, D)),
        'norm_ffn_post': gain(ks[9], (L, D)),
        'w_in': nrm(ks[10], (L, D, D_IN), D ** -0.5),
        'mla_q_norm': gain(ks[11], (L, MLA_Q_RANK)),
        'mla_kv_norm': gain(ks[12], (L, MLA_KV_RANK)),
        'mla_w_qb': nrm(ks[13], (L, MLA_Q_RANK, MLA_HEADS * (MLA_NOPE + MLA_ROPE)), MLA_Q_RANK ** -0.5),
        'mla_w_kvb': nrm(ks[14], (L, MLA_KV_RANK, MLA_HEADS * (MLA_NOPE + MLA_V)), MLA_KV_RANK ** -0.5),
        'gdn_conv_w': nrm(ks[15], (L, GDN_CONV, GDN_QKV), GDN_CONV ** -0.5),
        'gdn_a_log': jnp.log(jax.random.uniform(ks[17], (L, 2, GDN_HEADS), f32, 1.0, 16.0)),
        'gdn_dt_bias': dt + jnp.log(-jnp.expm1(-dt)),
        'gdn_o_norm': gain(ks[18], (L, GDN_DV)),
        'rg_conv_w': nrm(ks[19], (L, RG_CONV, RG_WIDTH), RG_CONV ** -0.5),
        'rg_conv_b': nrm(ks[20], (L, RG_WIDTH), 0.01),
        'rg_w_a': nrm(ks[21], (L, 2, RG_BLOCKS, bs, bs), bs ** -0.5),
        'rg_b_a': nrm(ks[22], (L, 2, RG_WIDTH), 0.01),
        'rg_w_x': nrm(ks[23], (L, 2, RG_BLOCKS, bs, bs), bs ** -0.5),
        'rg_b_x': nrm(ks[24], (L, 2, RG_WIDTH), 0.01),
        'rg_lambda': jnp.log(a0) - jnp.log1p(-a0),
        'w_out': nrm(ks[26], (L, D_MIX, D), D_MIX ** -0.5),
        'w_router': nrm(ks[27], (L, D, N_EXPERTS), D ** -0.5),
        'w_gate': nrm(ks[28], (L, N_EXPERTS, D, EXPERT_FF), D ** -0.5),
        'w_up': nrm(ks[29], (L, N_EXPERTS, D, EXPERT_FF), D ** -0.5),
        'w_down': nrm(ks[30], (L, N_EXPERTS, EXPERT_FF, D), EXPERT_FF ** -0.5),
    }


def reference(x, c, ctx, c_ctx, mod_w, mod_b, norm_mix_pre, norm_mix_post, norm_ffn_pre, norm_ffn_post,
              w_in, mla_q_norm, mla_kv_norm, mla_w_qb, mla_w_kvb, gdn_conv_w, gdn_a_log, gdn_dt_bias,
              gdn_o_norm, rg_conv_w, rg_conv_b, rg_w_a, rg_b_a, rg_w_x, rg_b_x, rg_lambda, w_out,
              w_router, w_gate, w_up, w_down):
    xc = ctx
    for l in range(DEPTH):
        p = {
            'mod_w': mod_w[l], 'mod_b': mod_b[l],
            'norm_mix_pre': norm_mix_pre[l], 'norm_mix_post': norm_mix_post[l],
            'norm_ffn_pre': norm_ffn_pre[l], 'norm_ffn_post': norm_ffn_post[l],
            'w_in': w_in[l],
            'mla_q_norm': mla_q_norm[l], 'mla_kv_norm': mla_kv_norm[l],
            'mla_w_qb': mla_w_qb[l], 'mla_w_kvb': mla_w_kvb[l],
            'gdn_conv_w': gdn_conv_w[l], 'gdn_a_log': gdn_a_log[l], 'gdn_dt_bias': gdn_dt_bias[l],
            'gdn_o_norm': gdn_o_norm[l],
            'rg_conv_w': rg_conv_w[l], 'rg_conv_b': rg_conv_b[l],
            'rg_w_a': rg_w_a[l], 'rg_b_a': rg_b_a[l], 'rg_w_x': rg_w_x[l], 'rg_b_x': rg_b_x[l],
            'rg_lambda': rg_lambda[l],
            'w_out': w_out[l],
            'w_router': w_router[l], 'w_gate': w_gate[l], 'w_up': w_up[l], 'w_down': w_down[l],
        }
        x, xc = hybrid_layer(x, xc, c, c_ctx, p, l < DEPTH - 1)
    return x
```

```python
import functools
import math

import jax
import jax.numpy as jnp
from jax import lax
from jax.experimental import pallas as pl
from jax.experimental.pallas import tpu as pltpu

F32 = jnp.float32
BF16 = jnp.bfloat16

D_MODEL = 4096
EPS = 1e-6
GRID_W = 64
ROPE_THETA = 10000.0

MLA_HEADS = 16
MLA_NOPE = 128
MLA_ROPE = 64
MLA_V = 128
MLA_Q_RANK = 1024
MLA_KV_RANK = 512

GDN_HEADS = 8
GDN_DK = 128
GDN_DV = 128
GDN_CHUNK = 64
GDN_W = GDN_HEADS * GDN_DK

RG_WIDTH = 1024
RG_BLOCKS = 8
RG_BS = RG_WIDTH // RG_BLOCKS
RG_C = 8.0

N_EXPERTS = 16
EXPERT_FF = 1024
CAPACITY_FACTOR = 2

LANE = 128
ROW_TILE = 256
VMEM_LIMIT = 56 * 1024 * 1024

Z_QA, Z_GQ, Z_GK, Z_GV, Z_GZ, Z_RX, Z_RY = 0, 1024, 2048, 3072, 4096, 5120, 6144
Z_KVA = 7168
Z_PEA, Z_PEB, Z_BA = 7680, 7808, 7936
Z_COLS = 8192


def _cparams(sem, vmem=VMEM_LIMIT):
    return pltpu.CompilerParams(dimension_semantics=sem, vmem_limit_bytes=vmem)


def _sigmoid(x):
    return 1.0 / (1.0 + jnp.exp(-x))


def _silu(x):
    return x * _sigmoid(x)


def _softplus(x):
    return jnp.maximum(x, 0.0) + jnp.log(1.0 + jnp.exp(-jnp.abs(x)))


def _one_minus_exp(y):
    poly = 1.0 + y * (1.0 / 13.0)
    for n in range(12, 1, -1):
        poly = 1.0 + (y * (1.0 / n)) * poly
    return jnp.where(y > -0.5, -(y * poly), 1.0 - jnp.exp(y))


def _gelu_tanh(x):
    return 0.5 * x * (1.0 + jnp.tanh(math.sqrt(2.0 / math.pi) * (x + 0.044715 * (x * x * x))))


def _rms(x, gain):
    return x * lax.rsqrt(jnp.mean(x * x, axis=-1, keepdims=True) + EPS) * gain


def _mod_kernel(c_ref, w_ref, b_ref, o_ref):
    s = _silu(c_ref[...])
    o_ref[...] = jnp.dot(s.astype(BF16), w_ref[...].astype(BF16), preferred_element_type=F32) + b_ref[...]


def mod_call(cc, w, b):
    tn = 1024
    n = w.shape[1]
    return pl.pallas_call(
        _mod_kernel,
        out_shape=jax.ShapeDtypeStruct((8, n), F32),
        grid=(n // tn,),
        in_specs=[pl.BlockSpec((8, D_MODEL), lambda j: (0, 0)),
                  pl.BlockSpec((D_MODEL, tn), lambda j: (0, j)),
                  pl.BlockSpec((1, tn), lambda j: (0, j))],
        out_specs=pl.BlockSpec((8, tn), lambda j: (0, j)),
        compiler_params=_cparams(("arbitrary",)),
        name="mod",
    )(cc, w, b)


def _prenorm_kernel(x_ref, g_ref, sc_ref, sh_ref, o_ref, *, ctx_tile):
    is_ctx = pl.program_id(0) >= ctx_tile
    sc = jnp.where(is_ctx, sc_ref[1:2, :], sc_ref[0:1, :])
    sh = jnp.where(is_ctx, sh_ref[1:2, :], sh_ref[0:1, :])
    y = _rms(x_ref[...], g_ref[...])
    o_ref[...] = (y * (1.0 + sc) + sh).astype(BF16)


def prenorm_call(x, gain, sc, sh, n_lat):
    t = x.shape[0]
    row = pl.BlockSpec((ROW_TILE, D_MODEL), lambda i: (i, 0))
    return pl.pallas_call(
        functools.partial(_prenorm_kernel, ctx_tile=n_lat // ROW_TILE),
        out_shape=jax.ShapeDtypeStruct((t, D_MODEL), BF16),
        grid=(t // ROW_TILE,),
        in_specs=[row, pl.BlockSpec((1, D_MODEL), lambda i: (0, 0)),
                  pl.BlockSpec((2, D_MODEL), lambda i: (0, 0)), pl.BlockSpec((2, D_MODEL), lambda i: (0, 0))],
        out_specs=row,
        compiler_params=_cparams(("arbitrary",)),
        name="prenorm",
    )(x, gain, sc, sh)


def _matmul_kernel(a_ref, w_ref, o_ref):
    o_ref[...] = jnp.dot(a_ref[...], w_ref[...].astype(BF16), preferred_element_type=F32)


def _pick_tm(m):
    for tm in (1056, 1024, 768, 512, 256):
        if m % tm == 0:
            return tm
    raise ValueError(f"unsupported row count {m}")


def matmul_call(a, w, tn=512, name="matmul"):
    m, k = a.shape
    n = w.shape[1]
    tm = _pick_tm(m)
    return pl.pallas_call(
        _matmul_kernel,
        out_shape=jax.ShapeDtypeStruct((m, n), F32),
        grid=(m // tm, n // tn),
        in_specs=[pl.BlockSpec((tm, k), lambda i, j: (i, 0)), pl.BlockSpec((k, tn), lambda i, j: (0, j))],
        out_specs=pl.BlockSpec((tm, tn), lambda i, j: (i, j)),
        compiler_params=_cparams(("arbitrary", "arbitrary")),
        name=name,
    )(a, w)


def _matmul3_kernel(a1_ref, a2_ref, a3_ref, w_ref, o_ref):
    k1 = a1_ref.shape[1]
    k2 = a2_ref.shape[1]
    acc = jnp.dot(a1_ref[...], w_ref[0:k1, :].astype(BF16), preferred_element_type=F32)
    acc += jnp.dot(a2_ref[...], w_ref[k1:k1 + k2, :].astype(BF16), preferred_element_type=F32)
    acc += jnp.dot(a3_ref[...], w_ref[k1 + k2:, :].astype(BF16), preferred_element_type=F32)
    o_ref[...] = acc


def matmul3_call(a1, a2, a3, w, m, tn=512):
    k = w.shape[0]
    n = w.shape[1]
    tm = _pick_tm(m)
    return pl.pallas_call(
        _matmul3_kernel,
        out_shape=jax.ShapeDtypeStruct((m, n), F32),
        grid=(m // tm, n // tn),
        in_specs=[pl.BlockSpec((tm, a1.shape[1]), lambda i, j: (i, 0)),
                  pl.BlockSpec((tm, a2.shape[1]), lambda i, j: (i, 0)),
                  pl.BlockSpec((tm, a3.shape[1]), lambda i, j: (i, 0)),
                  pl.BlockSpec((k, tn), lambda i, j: (0, j))],
        out_specs=pl.BlockSpec((tm, tn), lambda i, j: (i, j)),
        compiler_params=_cparams(("arbitrary", "arbitrary")),
        name="w_out",
    )(a1, a2, a3, w)


def _qproj_kernel(z_ref, g_ref, w_ref, c_ref, s_ref, o_ref, zn_ref, *, scale):
    @pl.when(pl.program_id(1) == 0)
    def _():
        zn_ref[...] = _rms(z_ref[...], g_ref[...]).astype(BF16)

    r = jnp.dot(zn_ref[...], w_ref[0].astype(BF16), preferred_element_type=F32)
    q0 = r[:, 0:LANE]
    q1 = r[:, LANE:2 * LANE] * c_ref[...] + r[:, 2 * LANE:3 * LANE] * s_ref[...]
    o_ref[0, :, 0:LANE] = (q0 * scale).astype(BF16)
    o_ref[0, :, LANE:2 * LANE] = (q1 * scale).astype(BF16)


def qproj_call(z, q_norm, wq, cq, sq):
    t = z.shape[0]
    tm = _pick_tm(t)
    scale = (MLA_NOPE + MLA_ROPE) ** -0.5
    return pl.pallas_call(
        functools.partial(_qproj_kernel, scale=scale),
        out_shape=jax.ShapeDtypeStruct((MLA_HEADS, t, 2 * LANE), BF16),
        grid=(t // tm, MLA_HEADS),
        in_specs=[pl.BlockSpec((tm, MLA_Q_RANK), lambda i, h: (i, Z_QA // MLA_Q_RANK)),
                  pl.BlockSpec((1, MLA_Q_RANK), lambda i, h: (0, 0)),
                  pl.BlockSpec((1, MLA_Q_RANK, 3 * LANE), lambda i, h: (h, 0, 0)),
                  pl.BlockSpec((tm, LANE), lambda i, h: (i, 0)),
                  pl.BlockSpec((tm, LANE), lambda i, h: (i, 0))],
        out_specs=pl.BlockSpec((1, tm, 2 * LANE), lambda i, h: (h, i, 0)),
        scratch_shapes=[pltpu.VMEM((tm, MLA_Q_RANK), BF16)],
        compiler_params=_cparams(("arbitrary", "arbitrary")),
        name="q_proj",
    )(z, q_norm, wq, cq, sq)


def _kvproj_kernel(z_ref, g_ref, w_ref, pa_ref, pb_ref, c_ref, s_ref, k_ref, v_ref, zn_ref):
    @pl.when(pl.program_id(1) == 0)
    def _():
        zn_ref[...] = _rms(z_ref[...], g_ref[...]).astype(BF16)

    r = jnp.dot(zn_ref[...], w_ref[...].astype(BF16), preferred_element_type=F32)
    k1 = pa_ref[...] * c_ref[...] + pb_ref[...] * s_ref[...]
    k_ref[0, :, 0:LANE] = r[:, 0:LANE].astype(BF16)
    k_ref[0, :, LANE:2 * LANE] = k1.astype(BF16)
    v_ref[0] = r[:, LANE:2 * LANE].astype(BF16)


def kvproj_call(z, kv_norm, wkv, ck, sk):
    t = z.shape[0]
    tm = _pick_tm(t)
    return pl.pallas_call(
        _kvproj_kernel,
        out_shape=(jax.ShapeDtypeStruct((MLA_HEADS, t, 2 * LANE), BF16),
                   jax.ShapeDtypeStruct((MLA_HEADS, t, LANE), BF16)),
        grid=(t // tm, MLA_HEADS),
        in_specs=[pl.BlockSpec((tm, MLA_KV_RANK), lambda i, h: (i, Z_KVA // MLA_KV_RANK)),
                  pl.BlockSpec((1, MLA_KV_RANK), lambda i, h: (0, 0)),
                  pl.BlockSpec((MLA_KV_RANK, 2 * LANE), lambda i, h: (0, h)),
                  pl.BlockSpec((tm, LANE), lambda i, h: (i, Z_PEA // LANE)),
                  pl.BlockSpec((tm, LANE), lambda i, h: (i, Z_PEB // LANE)),
                  pl.BlockSpec((tm, LANE), lambda i, h: (i, 0)),
                  pl.BlockSpec((tm, LANE), lambda i, h: (i, 0))],
        out_specs=(pl.BlockSpec((1, tm, 2 * LANE), lambda i, h: (h, i, 0)),
                   pl.BlockSpec((1, tm, LANE), lambda i, h: (h, i, 0))),
        scratch_shapes=[pltpu.VMEM((tm, MLA_KV_RANK), BF16)],
        compiler_params=_cparams(("arbitrary", "arbitrary")),
        name="kv_proj",
    )(z, kv_norm, wkv, z, z, ck, sk)


def _attn_kernel(q_ref, k_ref, v_ref, o_ref, m_ref, l_ref, acc_ref, *, tk, n_chunks):
    q = q_ref[0]
    m_ref[...] = jnp.full(m_ref.shape, -jnp.inf, F32)
    l_ref[...] = jnp.zeros(l_ref.shape, F32)
    acc_ref[...] = jnp.zeros(acc_ref.shape, F32)

    def body(c, carry):
        off = pl.multiple_of(c * tk, tk)
        k = k_ref[0, pl.ds(off, tk), :]
        v = v_ref[0, pl.ds(off, tk), :]
        s = lax.dot_general(q, k, (((1,), (1,)), ((), ())), preferred_element_type=F32)
        m_old = m_ref[...]
        m_new = jnp.maximum(m_old, jnp.max(s, axis=-1, keepdims=True))
        alpha = jnp.exp(m_old - m_new)
        p = jnp.exp(s - m_new)
        l_ref[...] = alpha * l_ref[...] + jnp.sum(p, axis=-1, keepdims=True)
        acc_ref[...] = alpha * acc_ref[...] + jnp.dot(p.astype(BF16), v, preferred_element_type=F32)
        m_ref[...] = m_new
        return carry

    lax.fori_loop(0, n_chunks, body, 0)
    o_ref[...] = (acc_ref[...] / l_ref[...]).astype(BF16)


def attn_call(q, k, v, n_q, q_row0, n_k, k_row0, tq, tk):
    assert n_q % tq == 0 and q_row0 % tq == 0 and n_k % tk == 0 and k_row0 % n_k == 0
    qb, kb = q_row0 // tq, k_row0 // n_k
    return pl.pallas_call(
        functools.partial(_attn_kernel, tk=tk, n_chunks=n_k // tk),
        out_shape=jax.ShapeDtypeStruct((n_q, MLA_HEADS * MLA_V), BF16),
        grid=(MLA_HEADS, n_q // tq),
        in_specs=[pl.BlockSpec((1, tq, 2 * LANE), lambda h, j: (h, j + qb, 0)),
                  pl.BlockSpec((1, n_k, 2 * LANE), lambda h, j: (h, kb, 0)),
                  pl.BlockSpec((1, n_k, LANE), lambda h, j: (h, kb, 0))],
        out_specs=pl.BlockSpec((tq, MLA_V), lambda h, j: (j, h)),
        scratch_shapes=[pltpu.VMEM((tq, 1), F32), pltpu.VMEM((tq, 1), F32), pltpu.VMEM((tq, MLA_V), F32)],
        compiler_params=_cparams(("arbitrary", "arbitrary")),
        name="mla_attn",
    )(q, k, v)


def _conv4(x_ref, p_ref, n_ref, w, ext_ref, has_prev, has_next):
    t = x_ref.shape[0]
    ext_ref[0:8, :] = jnp.where(has_prev, p_ref[...], 0.0)
    ext_ref[8:8 + t, :] = x_ref[...]
    ext_ref[8 + t:16 + t, :] = jnp.where(has_next, n_ref[...], 0.0)
    y = ext_ref[6:6 + t, :] * w[0:1, :]
    y += ext_ref[7:7 + t, :] * w[1:2, :]
    y += ext_ref[8:8 + t, :] * w[2:3, :]
    y += ext_ref[9:9 + t, :] * w[3:4, :]
    return y


def _halo_flags(i, n_lat_tiles):
    has_prev = jnp.logical_and(i > 0, i < n_lat_tiles)
    has_next = i < n_lat_tiles - 1
    return has_prev, has_next


def _halo_specs(width, col_of, n_tiles):
    r8 = ROW_TILE // 8
    last8 = n_tiles * r8 - 1
    main = pl.BlockSpec((ROW_TILE, width), lambda i, *a: (i, col_of(*a)))
    prev = pl.BlockSpec((8, width), lambda i, *a: (jnp.maximum(i * r8 - 1, 0), col_of(*a)))
    nxt = pl.BlockSpec((8, width), lambda i, *a: (jnp.minimum((i + 1) * r8, last8), col_of(*a)))
    return main, prev, nxt


def _gdn_prep_kernel(x_ref, p_ref, n_ref, w_ref, ba_ref, alog_ref, dtb_ref, o_ref, bg_ref, ext_ref, *, n_lat_tiles):
    i = pl.program_id(0)
    j = pl.program_id(1)
    has_prev, has_next = _halo_flags(i, n_lat_tiles)
    y = _silu(_conv4(x_ref, p_ref, n_ref, w_ref[...], ext_ref, has_prev, has_next))
    qscale = jnp.where(j == 0, GDN_DK ** -0.5, 1.0)
    is_v = j == 2
    for h in range(GDN_HEADS):
        seg = y[:, h * LANE:(h + 1) * LANE]
        nrm = seg * lax.rsqrt(jnp.sum(seg * seg, axis=-1, keepdims=True) + EPS) * qscale
        o_ref[0, :, h * LANE:(h + 1) * LANE] = jnp.where(is_v, seg, nrm)

    @pl.when(j == 0)
    def _():
        ba = ba_ref[...]
        lane = lax.broadcasted_iota(jnp.int32, ba.shape, 1)
        beta = _sigmoid(ba)
        g = -jnp.exp(alog_ref[...]) * _softplus(ba + dtb_ref[...])
        bg_ref[...] = jnp.where(lane < 2 * GDN_HEADS, beta, g)


def gdn_prep_call(z, conv_w, alog_row, dtb_row, n_lat):
    t = z.shape[0]
    n_tiles = t // ROW_TILE
    main, prev, nxt = _halo_specs(GDN_W, lambda j: Z_GQ // GDN_W + j, n_tiles)
    row128 = pl.BlockSpec((1, LANE), lambda i, j: (0, 0))
    return pl.pallas_call(
        functools.partial(_gdn_prep_kernel, n_lat_tiles=n_lat // ROW_TILE),
        out_shape=(jax.ShapeDtypeStruct((3, t, GDN_W), F32), jax.ShapeDtypeStruct((t, LANE), F32)),
        grid=(n_tiles, 3),
        in_specs=[main, prev, nxt,
                  pl.BlockSpec((4, GDN_W), lambda i, j: (0, j)),
                  pl.BlockSpec((ROW_TILE, LANE), lambda i, j: (i, Z_BA // LANE)),
                  row128, row128],
        out_specs=(pl.BlockSpec((1, ROW_TILE, GDN_W), lambda i, j: (j, i, 0)),
                   pl.BlockSpec((ROW_TILE, LANE), lambda i, j: (i, 0))),
        scratch_shapes=[pltpu.VMEM((ROW_TILE + 16, GDN_W), F32)],
        compiler_params=_cparams(("arbitrary", "arbitrary")),
        name="gdn_prep",
    )(z, z, z, conv_w, z, alog_row, dtb_row)


def _split3(x):
    hi = x.astype(BF16)
    r1 = x - hi.astype(F32)
    mid = r1.astype(BF16)
    lo = (r1 - mid.astype(F32)).astype(BF16)
    return hi, mid, lo


def _bdot(a, b):
    return jnp.dot(a.astype(BF16), b.astype(BF16), preferred_element_type=F32)


def _gdn_direction(q_ref, k_ref, v_ref, bg_ref, gt_ref, o_ref, s_ref, d):
    c = GDN_CHUNK
    row = lax.broadcasted_iota(jnp.int32, (c, c), 0)
    col = lax.broadcasted_iota(jnp.int32, (c, c), 1)
    if d == 0:
        incl = row >= col
        strict = row > col
    else:
        incl = row <= col
        strict = row < col
    incl_t = (row <= col) if d == 0 else (row >= col)
    tri = jnp.where(incl, 1.0, 0.0).astype(BF16)
    tri_t = jnp.where(incl_t, 1.0, 0.0).astype(BF16)
    bg = bg_ref[...]
    g_hi, g_mid, g_lo = _split3(bg)
    gc_all = (jnp.dot(tri, g_hi, preferred_element_type=F32) + jnp.dot(tri, g_mid, preferred_element_type=F32)
              + jnp.dot(tri, g_lo, preferred_element_type=F32))
    t_hi, t_mid, t_lo = _split3(gt_ref[0])
    gr_all = (jnp.dot(t_hi, tri_t, preferred_element_type=F32) + jnp.dot(t_mid, tri_t, preferred_element_type=F32)
              + jnp.dot(t_lo, tri_t, preferred_element_type=F32))
    eye = jnp.where(row == col, 1.0, 0.0)
    last = c - 1 if d == 0 else 0
    for h in range(GDN_HEADS):
        vh = d * GDN_HEADS + h
        sl = slice(h * LANE, (h + 1) * LANE)
        qh = q_ref[0, :, sl]
        kh = k_ref[0, :, sl]
        vv = v_ref[0, :, sl]
        beta = bg[:, vh:vh + 1]
        gcol = gc_all[:, 2 * GDN_HEADS + vh:2 * GDN_HEADS + vh + 1]
        grow = gr_all[vh:vh + 1, :]
        g_last = gcol[last:last + 1, :]
        decay = jnp.exp(jnp.where(incl, gcol - grow, -jnp.inf))
        kb = kh.astype(BF16)
        qk_kk = lax.dot_general(jnp.concatenate([qh.astype(BF16), kb], axis=0), kb,
                                (((1,), (1,)), ((), ())), preferred_element_type=F32)
        qk = qk_kk[0:c] * decay
        a = jnp.where(strict, qk_kk[c:2 * c] * beta * decay, 0.0)
        inv = eye - a
        pw = a
        for _ in range(5):
            pw = _bdot(pw, pw)
            inv = inv + _bdot(inv, pw)
        eg = jnp.exp(gcol)
        rhs = jnp.concatenate([vv * beta, kh * (beta * eg)], axis=1)
        uw = _bdot(inv, rhs)
        u = uw[:, 0:LANE]
        w = uw[:, LANE:2 * LANE]
        kt = kh * jnp.exp(g_last - gcol)
        qd = qh * eg
        s_old = s_ref[vh]
        ws_qs = _bdot(jnp.concatenate([w, qd], axis=0), s_old)
        v_new = u - ws_qs[0:c]
        o = ws_qs[c:2 * c] + _bdot(qk, v_new)
        s_ref[vh] = s_old * jnp.exp(g_last) + _bdot(kt.T, v_new)
        o_ref[:, sl] = o


def _gdn_kernel(qf_ref, kf_ref, vf_ref, bgf_ref, gtf_ref, qr_ref, kr_ref, vr_ref, bgr_ref, gtr_ref,
                of_ref, or_ref, s_ref):
    @pl.when(pl.program_id(0) == 0)
    def _():
        s_ref[...] = jnp.zeros(s_ref.shape, F32)

    _gdn_direction(qf_ref, kf_ref, vf_ref, bgf_ref, gtf_ref, of_ref, s_ref, 0)
    _gdn_direction(qr_ref, kr_ref, vr_ref, bgr_ref, gtr_ref, or_ref, s_ref, 1)


def gdn_call(qkv, bg, gt, n_lat):
    t = qkv.shape[1]
    c = GDN_CHUNK
    n = t // c
    n_ctx = (t - n_lat) // c

    def fwd(s):
        return (s + n - n_ctx) % n

    def rev(s):
        return n - 1 - s

    def specs(order):
        return [pl.BlockSpec((1, c, GDN_W), lambda s: (0, order(s), 0)),
                pl.BlockSpec((1, c, GDN_W), lambda s: (1, order(s), 0)),
                pl.BlockSpec((1, c, GDN_W), lambda s: (2, order(s), 0)),
                pl.BlockSpec((c, LANE), lambda s: (order(s), 0)),
                pl.BlockSpec((1, 2 * GDN_HEADS, c), lambda s: (order(s), 0, 0))]

    return pl.pallas_call(
        _gdn_kernel,
        out_shape=(jax.ShapeDtypeStruct((t, GDN_W), F32), jax.ShapeDtypeStruct((t, GDN_W), F32)),
        grid=(n,),
        in_specs=specs(fwd) + specs(rev),
        out_specs=(pl.BlockSpec((c, GDN_W), lambda s: (fwd(s), 0)), pl.BlockSpec((c, GDN_W), lambda s: (rev(s), 0))),
        scratch_shapes=[pltpu.VMEM((2 * GDN_HEADS, GDN_DK, GDN_DV), F32)],
        compiler_params=_cparams(("arbitrary",)),
        name="gdn_scan",
    )(qkv, qkv, qkv, bg, gt, qkv, qkv, qkv, bg, gt)


def _rg_direction(x_ref, p_ref, n_ref, cw_ref, cb_ref, wa_ref, ba_ref, wx_ref, bx_ref, lam_ref, o_ref,
                  h_ref, ext_ref, tile, n_lat_tiles, d):
    t = ROW_TILE
    has_prev, has_next = _halo_flags(tile, n_lat_tiles)
    x = _conv4(x_ref, p_ref, n_ref, cw_ref[...], ext_ref, has_prev, has_next) + cb_ref[...]
    xb = x.astype(BF16)
    r_parts, i_parts = [], []
    for n in range(RG_BLOCKS):
        seg = xb[:, n * RG_BS:(n + 1) * RG_BS]
        r_parts.append(jnp.dot(seg, wa_ref[0, n].astype(BF16), preferred_element_type=F32))
        i_parts.append(jnp.dot(seg, wx_ref[0, n].astype(BF16), preferred_element_type=F32))
    r = _sigmoid(jnp.concatenate(r_parts, axis=1) + ba_ref[0])
    ig = _sigmoid(jnp.concatenate(i_parts, axis=1) + bx_ref[0])
    log_a = -RG_C * r * _softplus(-lam_ref[0])
    a = jnp.exp(log_a)
    u = jnp.sqrt(_one_minus_exp(2.0 * log_a)) * (ig * x)
    rows = lax.broadcasted_iota(jnp.int32, (t, 1), 0)
    sh = 1
    while sh < t:
        if d == 0:
            a_s = pltpu.roll(a, sh, 0)
            u_s = pltpu.roll(u, sh, 0)
            ok = rows >= sh
        else:
            a_s = pltpu.roll(a, t - sh, 0)
            u_s = pltpu.roll(u, t - sh, 0)
            ok = rows < t - sh
        u = jnp.where(ok, a * u_s + u, u)
        a = jnp.where(ok, a * a_s, a)
        sh *= 2
    h = u + a * h_ref[d:d + 1, :]
    o_ref[...] = h
    edge = t - 1 if d == 0 else 0
    h_ref[d:d + 1, :] = h[edge:edge + 1, :]


def _rg_kernel(xf_ref, pf_ref, nf_ref, xr_ref, pr_ref, nr_ref, cw_ref, cb_ref,
               waf_ref, baf_ref, wxf_ref, bxf_ref, lamf_ref, war_ref, bar_ref, wxr_ref, bxr_ref, lamr_ref,
               of_ref, or_ref, h_ref, ext_ref, *, n_tiles, n_lat_tiles):
    s = pl.program_id(0)

    @pl.when(s == 0)
    def _():
        h_ref[...] = jnp.zeros(h_ref.shape, F32)

    tile_f = (s + n_lat_tiles) % n_tiles
    tile_r = n_tiles - 1 - s
    _rg_direction(xf_ref, pf_ref, nf_ref, cw_ref, cb_ref, waf_ref, baf_ref, wxf_ref, bxf_ref, lamf_ref, of_ref,
                  h_ref, ext_ref, tile_f, n_lat_tiles, 0)
    _rg_direction(xr_ref, pr_ref, nr_ref, cw_ref, cb_ref, war_ref, bar_ref, wxr_ref, bxr_ref, lamr_ref, or_ref,
                  h_ref, ext_ref, tile_r, n_lat_tiles, 1)


def rg_call(z, conv_w, conv_b, w_a, b_a, w_x, b_x, lam, n_lat):
    t = z.shape[0]
    n_tiles = t // ROW_TILE
    n_lat_tiles = n_lat // ROW_TILE
    r8 = ROW_TILE // 8
    last8 = n_tiles * r8 - 1
    colb = Z_RX // RG_WIDTH

    def fwd(s):
        return (s + n_lat_tiles) % n_tiles

    def rev(s):
        return n_tiles - 1 - s

    def tile_specs(order):
        return [pl.BlockSpec((ROW_TILE, RG_WIDTH), lambda s: (order(s), colb)),
                pl.BlockSpec((8, RG_WIDTH), lambda s: (jnp.maximum(order(s) * r8 - 1, 0), colb)),
                pl.BlockSpec((8, RG_WIDTH), lambda s: (jnp.minimum((order(s) + 1) * r8, last8), colb))]

    def dir_specs(d):
        return [pl.BlockSpec((1, RG_BLOCKS, RG_BS, RG_BS), lambda s: (d, 0, 0, 0)),
                pl.BlockSpec((1, 1, RG_WIDTH), lambda s: (d, 0, 0)),
                pl.BlockSpec((1, RG_BLOCKS, RG_BS, RG_BS), lambda s: (d, 0, 0, 0)),
                pl.BlockSpec((1, 1, RG_WIDTH), lambda s: (d, 0, 0)),
                pl.BlockSpec((1, 1, RG_WIDTH), lambda s: (d, 0, 0))]

    b_a3 = b_a.reshape(2, 1, RG_WIDTH)
    b_x3 = b_x.reshape(2, 1, RG_WIDTH)
    lam3 = lam.reshape(2, 1, RG_WIDTH)
    return pl.pallas_call(
        functools.partial(_rg_kernel, n_tiles=n_tiles, n_lat_tiles=n_lat_tiles),
        out_shape=(jax.ShapeDtypeStruct((t, RG_WIDTH), F32), jax.ShapeDtypeStruct((t, RG_WIDTH), F32)),
        grid=(n_tiles,),
        in_specs=tile_specs(fwd) + tile_specs(rev)
        + [pl.BlockSpec((4, RG_WIDTH), lambda s: (0, 0)), pl.BlockSpec((1, RG_WIDTH), lambda s: (0, 0))]
        + dir_specs(0) + dir_specs(1),
        out_specs=(pl.BlockSpec((ROW_TILE, RG_WIDTH), lambda s: (fwd(s), 0)),
                   pl.BlockSpec((ROW_TILE, RG_WIDTH), lambda s: (rev(s), 0))),
        scratch_shapes=[pltpu.VMEM((8, RG_WIDTH), F32), pltpu.VMEM((ROW_TILE + 16, RG_WIDTH), F32)],
        compiler_params=_cparams(("arbitrary",)),
        name="rg_lru",
    )(z, z, z, z, z, z, conv_w, conv_b, w_a, b_a3, w_x, b_x3, lam3, w_a, b_a3, w_x, b_x3, lam3)


def _assemble_kernel(of_ref, or_ref, gz_ref, on_ref, hf_ref, hr_ref, ry_ref, og_ref, org_ref):
    o = of_ref[...] + or_ref[...]
    gate = _silu(gz_ref[...])
    for h in range(GDN_HEADS):
        sl = slice(h * LANE, (h + 1) * LANE)
        seg = o[:, sl]
        on = seg * lax.rsqrt(jnp.mean(seg * seg, axis=-1, keepdims=True) + EPS) * on_ref[...]
        og_ref[:, sl] = (on * gate[:, sl]).astype(BF16)
    org_ref[...] = ((hf_ref[...] + hr_ref[...]) * _gelu_tanh(ry_ref[...])).astype(BF16)


def assemble_call(of, orv, z, o_norm, hf, hr, m):
    row = pl.BlockSpec((ROW_TILE, 1024), lambda i: (i, 0))
    return pl.pallas_call(
        _assemble_kernel,
        out_shape=(jax.ShapeDtypeStruct((m, GDN_W), BF16), jax.ShapeDtypeStruct((m, RG_WIDTH), BF16)),
        grid=(m // ROW_TILE,),
        in_specs=[row, row, pl.BlockSpec((ROW_TILE, GDN_W), lambda i: (i, Z_GZ // GDN_W)),
                  pl.BlockSpec((1, LANE), lambda i: (0, 0)), row, row,
                  pl.BlockSpec((ROW_TILE, RG_WIDTH), lambda i: (i, Z_RY // RG_WIDTH))],
        out_specs=(row, row),
        compiler_params=_cparams(("arbitrary",)),
        name="mix_assemble",
    )(of, orv, z, o_norm, hf, hr, z)


def _post_mix_kernel(y_ref, x_ref, gpost_ref, gm_ref, gpre_ref, sc_ref, sh_ref, wr_ref,
                     x1_ref, h2_ref, aff_ref, *, ctx_tile):
    is_ctx = pl.program_id(0) >= ctx_tile

    def pick(ref):
        return jnp.where(is_ctx, ref[1:2, :], ref[0:1, :])

    x1 = x_ref[...] + pick(gm_ref) * _rms(y_ref[...], gpost_ref[...])
    x1_ref[...] = x1
    h2 = _rms(x1, gpre_ref[...]) * (1.0 + pick(sc_ref)) + pick(sh_ref)
    h2b = h2.astype(BF16)
    h2_ref[...] = h2b
    h_lo = (h2 - h2b.astype(F32)).astype(BF16)
    wr = wr_ref[...]
    w_hi = wr.astype(BF16)
    w_lo = (wr - w_hi.astype(F32)).astype(BF16)
    nt = (((1,), (1,)), ((), ()))
    logits = (lax.dot_general(w_hi, h2b, nt, preferred_element_type=F32)
              + lax.dot_general(w_hi, h_lo, nt, preferred_element_type=F32)
              + lax.dot_general(w_lo, h2b, nt, preferred_element_type=F32))
    e = jnp.exp(logits - jnp.max(logits, axis=0, keepdims=True))
    aff_ref[...] = e / jnp.sum(e, axis=0, keepdims=True)


def post_mix_call(y, x, gpost, gm, gpre, sc, sh, wr_t, n_lat):
    m = y.shape[0]
    row = pl.BlockSpec((ROW_TILE, D_MODEL), lambda i: (i, 0))
    vec1 = pl.BlockSpec((1, D_MODEL), lambda i: (0, 0))
    vec2 = pl.BlockSpec((2, D_MODEL), lambda i: (0, 0))
    return pl.pallas_call(
        functools.partial(_post_mix_kernel, ctx_tile=n_lat // ROW_TILE),
        out_shape=(jax.ShapeDtypeStruct((m, D_MODEL), F32), jax.ShapeDtypeStruct((m, D_MODEL), BF16),
                   jax.ShapeDtypeStruct((N_EXPERTS, m), F32)),
        grid=(m // ROW_TILE,),
        in_specs=[row, row, vec1, vec2, vec1, vec2, vec2, pl.BlockSpec((N_EXPERTS, D_MODEL), lambda i: (0, 0))],
        out_specs=(row, row, pl.BlockSpec((N_EXPERTS, ROW_TILE), lambda i: (0, i))),
        compiler_params=_cparams(("arbitrary",)),
        name="post_mix",
    )(y, x, gpost, gm, gpre, sc, sh, wr_t)


def _post_ffn_kernel(y_ref, x_ref, g_ref, gf_ref, o_ref, *, ctx_tile):
    is_ctx = pl.program_id(0) >= ctx_tile
    gf = jnp.where(is_ctx, gf_ref[1:2, :], gf_ref[0:1, :])
    o_ref[...] = x_ref[...] + gf * _rms(y_ref[...], g_ref[...])


def post_ffn_call(y, x, gain, gf, n_lat):
    m = y.shape[0]
    row = pl.BlockSpec((ROW_TILE, D_MODEL), lambda i: (i, 0))
    return pl.pallas_call(
        functools.partial(_post_ffn_kernel, ctx_tile=n_lat // ROW_TILE),
        out_shape=jax.ShapeDtypeStruct((m, D_MODEL), F32),
        grid=(m // ROW_TILE,),
        in_specs=[row, row, pl.BlockSpec((1, D_MODEL), lambda i: (0, 0)), pl.BlockSpec((2, D_MODEL), lambda i: (0, 0))],
        out_specs=row,
        compiler_params=_cparams(("arbitrary",)),
        name="post_ffn",
    )(y, x, gain, gf)


def _moe_up_kernel(x_ref, wg_ref, wu_ref, o_ref):
    x = x_ref[0]
    g = jnp.dot(x, wg_ref[0].astype(BF16), preferred_element_type=F32)
    u = jnp.dot(x, wu_ref[0].astype(BF16), preferred_element_type=F32)
    o_ref[0] = (_silu(g) * u).astype(BF16)


def moe_up_call(xg, w_gate, w_up, tf=256):
    e, c, d = xg.shape
    return pl.pallas_call(
        _moe_up_kernel,
        out_shape=jax.ShapeDtypeStruct((e, c, EXPERT_FF), BF16),
        grid=(e, EXPERT_FF // tf),
        in_specs=[pl.BlockSpec((1, c, d), lambda i, f: (i, 0, 0)),
                  pl.BlockSpec((1, d, tf), lambda i, f: (i, 0, f)),
                  pl.BlockSpec((1, d, tf), lambda i, f: (i, 0, f))],
        out_specs=pl.BlockSpec((1, c, tf), lambda i, f: (i, 0, f)),
        compiler_params=_cparams(("arbitrary", "arbitrary")),
        name="moe_up",
    )(xg, w_gate, w_up)


def _moe_down_kernel(h_ref, wd_ref, g_ref, o_ref):
    y = jnp.dot(h_ref[0], wd_ref[0].astype(BF16), preferred_element_type=F32)
    o_ref[0] = y * g_ref[0]


def moe_down_call(hid, w_down, gates, tn=512):
    e, c, f = hid.shape
    d = w_down.shape[2]
    return pl.pallas_call(
        _moe_down_kernel,
        out_shape=jax.ShapeDtypeStruct((e, c, d), F32),
        grid=(e, d // tn),
        in_specs=[pl.BlockSpec((1, c, f), lambda i, n: (i, 0, 0)),
                  pl.BlockSpec((1, f, tn), lambda i, n: (i, 0, n)),
                  pl.BlockSpec((1, c, 1), lambda i, n: (i, 0, 0))],
        out_specs=pl.BlockSpec((1, c, tn), lambda i, n: (i, 0, n)),
        compiler_params=_cparams(("arbitrary", "arbitrary")),
        name="moe_down",
    )(hid, w_down, gates)


def _rope_tables(n_lat, n_ctx):
    rows = n_lat // GRID_W
    row = jnp.broadcast_to(jnp.arange(rows, dtype=F32)[:, None], (rows, GRID_W)).reshape(-1)
    col = jnp.broadcast_to(jnp.arange(GRID_W, dtype=F32)[None, :], (rows, GRID_W)).reshape(-1)
    n_freq = MLA_ROPE // 4
    inv_freq = ROPE_THETA ** (-jnp.arange(n_freq, dtype=F32) / n_freq)
    ang = jnp.concatenate([row[:, None] * inv_freq, col[:, None] * inv_freq], axis=-1)
    cos = jnp.cos(ang)
    sin = jnp.sin(ang)
    cos64 = jnp.concatenate([cos, cos], axis=-1)
    sin64 = jnp.concatenate([sin, sin], axis=-1)
    one_l = jnp.ones((n_lat, 64), F32)
    zero_l = jnp.zeros((n_lat, 64), F32)
    one_c = jnp.ones((n_ctx, 64), F32)
    zero_c = jnp.zeros((n_ctx, 64), F32)
    cq = jnp.concatenate([jnp.concatenate([cos64, one_l], 1), jnp.concatenate([one_c, one_c], 1)], 0)
    sq = jnp.concatenate([jnp.concatenate([sin64, zero_l], 1), jnp.concatenate([zero_c, zero_c], 1)], 0)
    ck = jnp.concatenate([jnp.concatenate([cos64, zero_l], 1), jnp.concatenate([zero_c, one_c], 1)], 0)
    sk = jnp.concatenate([jnp.concatenate([sin64, zero_l], 1), jnp.concatenate([zero_c, zero_c], 1)], 0)
    return cq, sq, ck, sk


def _rot_half_cols(w):
    half = MLA_ROPE // 2
    return jnp.concatenate([-w[..., half:], w[..., :half]], axis=-1)


def _relayout_w_in(w_in):
    q_a = w_in[:, 0:1024]
    kv_a = w_in[:, 1024:1536]
    k_pe = w_in[:, 1536:1600]
    g_qkv = w_in[:, 1600:4672]
    g_z = w_in[:, 4672:5696]
    g_ba = w_in[:, 5696:5728]
    r_x = w_in[:, 5728:6752]
    r_y = w_in[:, 6752:7776]
    zeros = functools.partial(jnp.zeros, dtype=w_in.dtype)
    d = w_in.shape[0]
    return jnp.concatenate([q_a, g_qkv, g_z, r_x, r_y, kv_a, k_pe, k_pe, _rot_half_cols(k_pe), zeros((d, 64)),
                            g_ba, zeros((d, LANE - 32)), zeros((d, LANE))], axis=1)


def _relayout_w_qb(w_qb):
    w = w_qb.reshape(MLA_Q_RANK, MLA_HEADS, MLA_NOPE + MLA_ROPE)
    nope = w[:, :, :MLA_NOPE]
    pe = w[:, :, MLA_NOPE:]
    zeros = jnp.zeros_like(pe)
    out = jnp.concatenate([nope, pe, pe, _rot_half_cols(pe), zeros], axis=-1)
    return jnp.transpose(out, (1, 0, 2))


def _layer(x_all, c_rows, p, n_lat, n_ctx, update_ctx):
    t = n_lat + n_ctx
    mod = mod_call(c_rows, p['mod_w'], p['mod_b'][None, :])[0:2]
    sh_m, sc_m, g_m, sh_f, sc_f, g_f = [mod[:, i * D_MODEL:(i + 1) * D_MODEL] for i in range(6)]

    h = prenorm_call(x_all, p['norm_mix_pre'][None, :], sc_m, sh_m, n_lat)
    z = matmul_call(h, _relayout_w_in(p['w_in']), name="w_in")

    cq, sq, ck, sk = _rope_tables(n_lat, n_ctx)
    q = qproj_call(z, p['mla_q_norm'][None, :], _relayout_w_qb(p['mla_w_qb']), cq, sq)
    k, v = kvproj_call(z, p['mla_kv_norm'][None, :], p['mla_w_kvb'], ck, sk)
    tk = next(c for c in (1024, 768, 512, 256) if t % c == 0)
    o_mla = attn_call(q, k, v, n_lat, 0, t, 0, tq=512, tk=tk)
    if update_ctx:
        o_mla_c = attn_call(q, k, v, n_ctx, n_lat, n_ctx, n_lat, tq=n_ctx, tk=n_ctx)
        o_mla = jnp.concatenate([o_mla, o_mla_c], axis=0)

    pad = jnp.zeros((2 * GDN_HEADS,), F32)
    alog_row = jnp.concatenate([pad, p['gdn_a_log'].reshape(-1), jnp.zeros((LANE - 4 * GDN_HEADS,), F32)])[None, :]
    dtb_row = jnp.concatenate([pad, p['gdn_dt_bias'].reshape(-1), jnp.zeros((LANE - 4 * GDN_HEADS,), F32)])[None, :]
    qkv, bg = gdn_prep_call(z, p['gdn_conv_w'], alog_row, dtb_row, n_lat)
    gt = jnp.swapaxes(bg[:, 2 * GDN_HEADS:4 * GDN_HEADS].reshape(t // GDN_CHUNK, GDN_CHUNK, 2 * GDN_HEADS), 1, 2)
    o_f, o_r = gdn_call(qkv, bg, gt, n_lat)

    h_f, h_r = rg_call(z, p['rg_conv_w'], p['rg_conv_b'][None, :], p['rg_w_a'], p['rg_b_a'], p['rg_w_x'],
                       p['rg_b_x'], p['rg_lambda'], n_lat)

    m = t if update_ctx else n_lat
    o_gdn, o_rg = assemble_call(o_f, o_r, z, p['gdn_o_norm'][None, :], h_f, h_r, m)
    y = matmul3_call(o_mla, o_gdn, o_rg, p['w_out'], m)
    x1, h2, aff_t = post_mix_call(y, x_all, p['norm_mix_post'][None, :], g_m, p['norm_ffn_pre'][None, :],
                                  sc_f, sh_f, p['w_router'].T, n_lat)

    cap = CAPACITY_FACTOR * n_lat // N_EXPERTS
    gates, idx = lax.top_k(aff_t[:, :n_lat], cap)
    if update_ctx:
        cap_c = CAPACITY_FACTOR * n_ctx // N_EXPERTS
        gates_c, idx_c = lax.top_k(aff_t[:, n_lat:], cap_c)
        gates = jnp.concatenate([gates, gates_c], axis=1)
        idx = jnp.concatenate([idx, idx_c + n_lat], axis=1)
    xg = jnp.take(h2, idx, axis=0)
    hid = moe_up_call(xg, p['w_gate'], p['w_up'])
    y_e = moe_down_call(hid, p['w_down'], gates[..., None])
    y_moe = jnp.zeros((m, D_MODEL), F32).at[idx.reshape(-1)].add(y_e.reshape(-1, D_MODEL))
    return post_ffn_call(y_moe, x1, p['norm_ffn_post'][None, :], g_f, n_lat)


def kernel(x, c, ctx, c_ctx, mod_w, mod_b, norm_mix_pre, norm_mix_post, norm_ffn_pre, norm_ffn_post,
           w_in, mla_q_norm, mla_kv_norm, mla_w_qb, mla_w_kvb, gdn_conv_w, gdn_a_log, gdn_dt_bias,
           gdn_o_norm, rg_conv_w, rg_conv_b, rg_w_a, rg_b_a, rg_w_x, rg_b_x, rg_lambda, w_out,
           w_router, w_gate, w_up, w_down):
    assert x.shape[0] == 1 and ctx.shape[0] == 1
    n_lat, n_ctx = x.shape[1], ctx.shape[1]
    assert n_ctx == ROW_TILE and n_lat % 512 == 0
    depth = mod_w.shape[0]
    stacked = dict(mod_w=mod_w, mod_b=mod_b, norm_mix_pre=norm_mix_pre, norm_mix_post=norm_mix_post,
                   norm_ffn_pre=norm_ffn_pre, norm_ffn_post=norm_ffn_post, w_in=w_in, mla_q_norm=mla_q_norm,
                   mla_kv_norm=mla_kv_norm, mla_w_qb=mla_w_qb, mla_w_kvb=mla_w_kvb, gdn_conv_w=gdn_conv_w,
                   gdn_a_log=gdn_a_log, gdn_dt_bias=gdn_dt_bias, gdn_o_norm=gdn_o_norm, rg_conv_w=rg_conv_w,
                   rg_conv_b=rg_conv_b, rg_w_a=rg_w_a, rg_b_a=rg_b_a, rg_w_x=rg_w_x, rg_b_x=rg_b_x,
                   rg_lambda=rg_lambda, w_out=w_out, w_router=w_router, w_gate=w_gate, w_up=w_up, w_down=w_down)
    c_rows = jnp.concatenate([c, c_ctx[None, :], jnp.zeros((6, D_MODEL), F32)], axis=0)
    x_all = jnp.concatenate([x[0], ctx[0]], axis=0)
    for l in range(depth):
        p = {k: v[l] for k, v in stacked.items()}
        x_all = _layer(x_all, c_rows, p, n_lat, n_ctx, l < depth - 1)
    return x_all[None]
```

```python
import functools
import math

import jax
import jax.numpy as jnp
from jax import lax
from jax.experimental import pallas as pl
from jax.experimental.pallas import tpu as pltpu

F32 = jnp.float32
BF16 = jnp.bfloat16

D_MODEL = 4096
EPS = 1e-6
GRID_W = 64
ROPE_THETA = 10000.0

MLA_HEADS = 16
MLA_NOPE = 128
MLA_ROPE = 64
MLA_V = 128
MLA_Q_RANK = 1024
MLA_KV_RANK = 512

GDN_HEADS = 8
GDN_DK = 128
GDN_DV = 128
GDN_CHUNK = 64
GDN_W = GDN_HEADS * GDN_DK

RG_WIDTH = 1024
RG_BLOCKS = 8
RG_BS = RG_WIDTH // RG_BLOCKS
RG_C = 8.0

N_EXPERTS = 16
EXPERT_FF = 1024
CAPACITY_FACTOR = 2

LANE = 128
ROW_TILE = 256
VMEM_LIMIT = 56 * 1024 * 1024

Z_QA, Z_GQ, Z_GK, Z_GV, Z_GZ, Z_RX, Z_RY = 0, 1024, 2048, 3072, 4096, 5120, 6144
Z_KVA = 7168
Z_PEA, Z_PEB, Z_BA = 7680, 7808, 7936
Z_COLS = 8192


def _cparams(sem, vmem=VMEM_LIMIT):
    return pltpu.CompilerParams(dimension_semantics=sem, vmem_limit_bytes=vmem)


def _sigmoid(x):
    return 1.0 / (1.0 + jnp.exp(-x))


def _silu(x):
    return x * _sigmoid(x)


def _softplus(x):
    return jnp.maximum(x, 0.0) + jnp.log(1.0 + jnp.exp(-jnp.abs(x)))


def _one_minus_exp(y):
    poly = 1.0 + y * (1.0 / 13.0)
    for n in range(12, 1, -1):
        poly = 1.0 + (y * (1.0 / n)) * poly
    return jnp.where(y > -0.5, -(y * poly), 1.0 - jnp.exp(y))


def _gelu_tanh(x):
    return 0.5 * x * (1.0 + jnp.tanh(math.sqrt(2.0 / math.pi) * (x + 0.044715 * (x * x * x))))


def _rms(x, gain):
    return x * lax.rsqrt(jnp.mean(x * x, axis=-1, keepdims=True) + EPS) * gain


def _mod_kernel(c_ref, w_ref, b_ref, o_ref):
    s = _silu(c_ref[...])
    o_ref[...] = jnp.dot(s.astype(BF16), w_ref[...].astype(BF16), preferred_element_type=F32) + b_ref[...]


def mod_call(cc, w, b):
    tn = 1024
    n = w.shape[1]
    return pl.pallas_call(
        _mod_kernel,
        out_shape=jax.ShapeDtypeStruct((8, n), F32),
        grid=(n // tn,),
        in_specs=[pl.BlockSpec((8, D_MODEL), lambda j: (0, 0)),
                  pl.BlockSpec((D_MODEL, tn), lambda j: (0, j)),
                  pl.BlockSpec((1, tn), lambda j: (0, j))],
        out_specs=pl.BlockSpec((8, tn), lambda j: (0, j)),
        compiler_params=_cparams(("arbitrary",)),
        name="mod",
    )(cc, w, b)


def _prenorm_kernel(x_ref, g_ref, sc_ref, sh_ref, o_ref, *, ctx_tile):
    is_ctx = pl.program_id(0) >= ctx_tile
    sc = jnp.where(is_ctx, sc_ref[1:2, :], sc_ref[0:1, :])
    sh = jnp.where(is_ctx, sh_ref[1:2, :], sh_ref[0:1, :])
    y = _rms(x_ref[...], g_ref[...])
    o_ref[...] = (y * (1.0 + sc) + sh).astype(BF16)


def prenorm_call(x, gain, sc, sh, n_lat):
    t = x.shape[0]
    row = pl.BlockSpec((ROW_TILE, D_MODEL), lambda i: (i, 0))
    return pl.pallas_call(
        functools.partial(_prenorm_kernel, ctx_tile=n_lat // ROW_TILE),
        out_shape=jax.ShapeDtypeStruct((t, D_MODEL), BF16),
        grid=(t // ROW_TILE,),
        in_specs=[row, pl.BlockSpec((1, D_MODEL), lambda i: (0, 0)),
                  pl.BlockSpec((2, D_MODEL), lambda i: (0, 0)), pl.BlockSpec((2, D_MODEL), lambda i: (0, 0))],
        out_specs=row,
        compiler_params=_cparams(("arbitrary",)),
        name="prenorm",
    )(x, gain, sc, sh)


def _matmul_kernel(a_ref, w_ref, o_ref):
    o_ref[...] = jnp.dot(a_ref[...], w_ref[...].astype(BF16), preferred_element_type=F32)


def _pick_tm(m):
    for tm in (1056, 1024, 768, 512, 256):
        if m % tm == 0:
            return tm
    raise ValueError(f"unsupported row count {m}")


def matmul_call(a, w, tn=512, name="matmul"):
    m, k = a.shape
    n = w.shape[1]
    tm = _pick_tm(m)
    return pl.pallas_call(
        _matmul_kernel,
        out_shape=jax.ShapeDtypeStruct((m, n), F32),
        grid=(m // tm, n // tn),
        in_specs=[pl.BlockSpec((tm, k), lambda i, j: (i, 0)), pl.BlockSpec((k, tn), lambda i, j: (0, j))],
        out_specs=pl.BlockSpec((tm, tn), lambda i, j: (i, j)),
        compiler_params=_cparams(("arbitrary", "arbitrary")),
        name=name,
    )(a, w)


def _matmul3_kernel(a1_ref, a2_ref, a3_ref, w_ref, o_ref):
    k1 = a1_ref.shape[1]
    k2 = a2_ref.shape[1]
    acc = jnp.dot(a1_ref[...], w_ref[0:k1, :].astype(BF16), preferred_element_type=F32)
    acc += jnp.dot(a2_ref[...], w_ref[k1:k1 + k2, :].astype(BF16), preferred_element_type=F32)
    acc += jnp.dot(a3_ref[...], w_ref[k1 + k2:, :].astype(BF16), preferred_element_type=F32)
    o_ref[...] = acc


def matmul3_call(a1, a2, a3, w, m, tn=512):
    k = w.shape[0]
    n = w.shape[1]
    tm = _pick_tm(m)
    return pl.pallas_call(
        _matmul3_kernel,
        out_shape=jax.ShapeDtypeStruct((m, n), F32),
        grid=(m // tm, n // tn),
        in_specs=[pl.BlockSpec((tm, a1.shape[1]), lambda i, j: (i, 0)),
                  pl.BlockSpec((tm, a2.shape[1]), lambda i, j: (i, 0)),
                  pl.BlockSpec((tm, a3.shape[1]), lambda i, j: (i, 0)),
                  pl.BlockSpec((k, tn), lambda i, j: (0, j))],
        out_specs=pl.BlockSpec((tm, tn), lambda i, j: (i, j)),
        compiler_params=_cparams(("arbitrary", "arbitrary")),
        name="w_out",
    )(a1, a2, a3, w)


def _qproj_kernel(z_ref, g_ref, w_ref, c_ref, s_ref, o_ref, zn_ref, *, scale):
    @pl.when(pl.program_id(1) == 0)
    def _():
        zn_ref[...] = _rms(z_ref[...], g_ref[...]).astype(BF16)

    r = jnp.dot(zn_ref[...], w_ref[0].astype(BF16), preferred_element_type=F32)
    q0 = r[:, 0:LANE]
    q1 = r[:, LANE:2 * LANE] * c_ref[...] + r[:, 2 * LANE:3 * LANE] * s_ref[...]
    o_ref[0, :, 0:LANE] = (q0 * scale).astype(BF16)
    o_ref[0, :, LANE:2 * LANE] = (q1 * scale).astype(BF16)


def qproj_call(z, q_norm, wq, cq, sq):
    t = z.shape[0]
    tm = _pick_tm(t)
    scale = (MLA_NOPE + MLA_ROPE) ** -0.5 * math.log2(math.e)
    return pl.pallas_call(
        functools.partial(_qproj_kernel, scale=scale),
        out_shape=jax.ShapeDtypeStruct((MLA_HEADS, t, 2 * LANE), BF16),
        grid=(t // tm, MLA_HEADS),
        in_specs=[pl.BlockSpec((tm, MLA_Q_RANK), lambda i, h: (i, Z_QA // MLA_Q_RANK)),
                  pl.BlockSpec((1, MLA_Q_RANK), lambda i, h: (0, 0)),
                  pl.BlockSpec((1, MLA_Q_RANK, 3 * LANE), lambda i, h: (h, 0, 0)),
                  pl.BlockSpec((tm, LANE), lambda i, h: (i, 0)),
                  pl.BlockSpec((tm, LANE), lambda i, h: (i, 0))],
        out_specs=pl.BlockSpec((1, tm, 2 * LANE), lambda i, h: (h, i, 0)),
        scratch_shapes=[pltpu.VMEM((tm, MLA_Q_RANK), BF16)],
        compiler_params=_cparams(("arbitrary", "arbitrary")),
        name="q_proj",
    )(z, q_norm, wq, cq, sq)


def _kvproj_kernel(z_ref, g_ref, w_ref, pa_ref, pb_ref, c_ref, s_ref, k_ref, v_ref, zn_ref):
    @pl.when(pl.program_id(1) == 0)
    def _():
        zn_ref[...] = _rms(z_ref[...], g_ref[...]).astype(BF16)

    r = jnp.dot(zn_ref[...], w_ref[...].astype(BF16), preferred_element_type=F32)
    k1 = pa_ref[...] * c_ref[...] + pb_ref[...] * s_ref[...]
    k_ref[0, :, 0:LANE] = r[:, 0:LANE].astype(BF16)
    k_ref[0, :, LANE:2 * LANE] = k1.astype(BF16)
    v_ref[0, 0] = r[:, LANE:2 * LANE].T.astype(BF16)


def kvproj_call(z, kv_norm, wkv, ck, sk, tm):
    t = z.shape[0]
    return pl.pallas_call(
        _kvproj_kernel,
        out_shape=(jax.ShapeDtypeStruct((MLA_HEADS, t, 2 * LANE), BF16),
                   jax.ShapeDtypeStruct((MLA_HEADS, t // tm, LANE, tm), BF16)),
        grid=(t // tm, MLA_HEADS),
        in_specs=[pl.BlockSpec((tm, MLA_KV_RANK), lambda i, h: (i, Z_KVA // MLA_KV_RANK)),
                  pl.BlockSpec((1, MLA_KV_RANK), lambda i, h: (0, 0)),
                  pl.BlockSpec((MLA_KV_RANK, 2 * LANE), lambda i, h: (0, h)),
                  pl.BlockSpec((tm, LANE), lambda i, h: (i, Z_PEA // LANE)),
                  pl.BlockSpec((tm, LANE), lambda i, h: (i, Z_PEB // LANE)),
                  pl.BlockSpec((tm, LANE), lambda i, h: (i, 0)),
                  pl.BlockSpec((tm, LANE), lambda i, h: (i, 0))],
        out_specs=(pl.BlockSpec((1, tm, 2 * LANE), lambda i, h: (h, i, 0)),
                   pl.BlockSpec((1, 1, LANE, tm), lambda i, h: (h, i, 0, 0))),
        scratch_shapes=[pltpu.VMEM((tm, MLA_KV_RANK), BF16)],
        compiler_params=_cparams(("arbitrary", "arbitrary")),
        name="kv_proj",
    )(z, kv_norm, wkv, z, z, ck, sk)


def _attn_kernel(q_ref, k_ref, vt_ref, o_ref, s0_ref, s1_ref, m_ref, l_ref, acc_ref, *, tk, n_chunks):
    q = q_ref[0]
    nt = (((1,), (1,)), ((), ()))

    def scores(c):
        return lax.dot_general(k_ref[0, c * tk:(c + 1) * tk, :], q, nt, preferred_element_type=F32)

    def consume(c, s_ref):
        m_old = m_ref[...]
        m_new = jnp.maximum(m_old, jnp.max(s_ref[...], axis=0, keepdims=True))
        alpha = jnp.exp2(m_old - m_new)
        p = jnp.exp2(s_ref[...] - m_new)
        l_ref[...] = alpha * l_ref[...] + jnp.sum(p, axis=0, keepdims=True)
        pv = jnp.dot(vt_ref[0, c], p.astype(BF16), preferred_element_type=F32)
        acc_ref[...] = alpha * acc_ref[...] + pv
        m_ref[...] = m_new

    m_ref[...] = jnp.full(m_ref.shape, -jnp.inf, F32)
    l_ref[...] = jnp.zeros(l_ref.shape, F32)
    acc_ref[...] = jnp.zeros(acc_ref.shape, F32)
    bufs = (s0_ref, s1_ref)
    s0_ref[...] = scores(0)
    for c in range(n_chunks):
        if c + 1 < n_chunks:
            bufs[(c + 1) % 2][...] = scores(c + 1)
        consume(c, bufs[c % 2])
    o_ref[...] = (acc_ref[...] / l_ref[...]).T.astype(BF16)


def attn_call(q, k, vt, n_q, q_row0, n_k, k_row0, tq, tk):
    ck = vt.shape[3]
    assert n_q % tq == 0 and q_row0 % tq == 0 and n_k % tk == 0 and k_row0 % n_k == 0
    n_chunks = n_k // tk
    if n_chunks > 1:
        assert tk == ck and k_row0 == 0
        vt_spec = pl.BlockSpec((1, n_chunks, MLA_V, ck), lambda h, j: (h, 0, 0, 0))
    else:
        assert (k_row0 % ck) % tk == 0
        vt_spec = pl.BlockSpec((1, 1, MLA_V, tk), lambda h, j: (h, k_row0 // ck, 0, (k_row0 % ck) // tk))
    qb, kb = q_row0 // tq, k_row0 // n_k
    return pl.pallas_call(
        functools.partial(_attn_kernel, tk=tk, n_chunks=n_chunks),
        out_shape=jax.ShapeDtypeStruct((n_q, MLA_HEADS * MLA_V), BF16),
        grid=(MLA_HEADS, n_q // tq),
        in_specs=[pl.BlockSpec((1, tq, 2 * LANE), lambda h, j: (h, j + qb, 0)),
                  pl.BlockSpec((1, n_k, 2 * LANE), lambda h, j: (h, kb, 0)),
                  vt_spec],
        out_specs=pl.BlockSpec((tq, MLA_V), lambda h, j: (j, h)),
        scratch_shapes=[pltpu.VMEM((tk, tq), F32), pltpu.VMEM((tk, tq), F32),
                        pltpu.VMEM((1, tq), F32), pltpu.VMEM((1, tq), F32),
                        pltpu.VMEM((MLA_V, tq), F32)],
        compiler_params=_cparams(("arbitrary", "arbitrary")),
        name="mla_attn",
    )(q, k, vt)


def _conv4(x_ref, p_ref, n_ref, w, ext_ref, has_prev, has_next):
    t = x_ref.shape[0]
    ext_ref[0:8, :] = jnp.where(has_prev, p_ref[...], 0.0)
    ext_ref[8:8 + t, :] = x_ref[...]
    ext_ref[8 + t:16 + t, :] = jnp.where(has_next, n_ref[...], 0.0)
    y = ext_ref[6:6 + t, :] * w[0:1, :]
    y += ext_ref[7:7 + t, :] * w[1:2, :]
    y += ext_ref[8:8 + t, :] * w[2:3, :]
    y += ext_ref[9:9 + t, :] * w[3:4, :]
    return y


def _halo_flags(i, n_lat_tiles):
    has_prev = jnp.logical_and(i > 0, i < n_lat_tiles)
    has_next = i < n_lat_tiles - 1
    return has_prev, has_next


def _halo_specs(width, col_of, n_tiles):
    r8 = ROW_TILE // 8
    last8 = n_tiles * r8 - 1
    main = pl.BlockSpec((ROW_TILE, width), lambda i, *a: (i, col_of(*a)))
    prev = pl.BlockSpec((8, width), lambda i, *a: (jnp.maximum(i * r8 - 1, 0), col_of(*a)))
    nxt = pl.BlockSpec((8, width), lambda i, *a: (jnp.minimum((i + 1) * r8, last8), col_of(*a)))
    return main, prev, nxt


def _gdn_prep_kernel(x_ref, p_ref, n_ref, w_ref, ba_ref, alog_ref, dtb_ref, o_ref, bg_ref, ext_ref, *, n_lat_tiles):
    i = pl.program_id(0)
    j = pl.program_id(1)
    has_prev, has_next = _halo_flags(i, n_lat_tiles)
    y = _silu(_conv4(x_ref, p_ref, n_ref, w_ref[...], ext_ref, has_prev, has_next))
    qscale = jnp.where(j == 0, GDN_DK ** -0.5, 1.0)
    is_v = j == 2
    for h in range(GDN_HEADS):
        seg = y[:, h * LANE:(h + 1) * LANE]
        nrm = seg * lax.rsqrt(jnp.sum(seg * seg, axis=-1, keepdims=True) + EPS) * qscale
        o_ref[0, :, h * LANE:(h + 1) * LANE] = jnp.where(is_v, seg, nrm)

    @pl.when(j == 0)
    def _():
        ba = ba_ref[...]
        lane = lax.broadcasted_iota(jnp.int32, ba.shape, 1)
        beta = _sigmoid(ba)
        g = -jnp.exp(alog_ref[...]) * _softplus(ba + dtb_ref[...])
        bg_ref[...] = jnp.where(lane < 2 * GDN_HEADS, beta, g)


def gdn_prep_call(z, conv_w, alog_row, dtb_row, n_lat):
    t = z.shape[0]
    n_tiles = t // ROW_TILE
    main, prev, nxt = _halo_specs(GDN_W, lambda j: Z_GQ // GDN_W + j, n_tiles)
    row128 = pl.BlockSpec((1, LANE), lambda i, j: (0, 0))
    return pl.pallas_call(
        functools.partial(_gdn_prep_kernel, n_lat_tiles=n_lat // ROW_TILE),
        out_shape=(jax.ShapeDtypeStruct((3, t, GDN_W), F32), jax.ShapeDtypeStruct((t, LANE), F32)),
        grid=(n_tiles, 3),
        in_specs=[main, prev, nxt,
                  pl.BlockSpec((4, GDN_W), lambda i, j: (0, j)),
                  pl.BlockSpec((ROW_TILE, LANE), lambda i, j: (i, Z_BA // LANE)),
                  row128, row128],
        out_specs=(pl.BlockSpec((1, ROW_TILE, GDN_W), lambda i, j: (j, i, 0)),
                   pl.BlockSpec((ROW_TILE, LANE), lambda i, j: (i, 0))),
        scratch_shapes=[pltpu.VMEM((ROW_TILE + 16, GDN_W), F32)],
        compiler_params=_cparams(("arbitrary", "arbitrary")),
        name="gdn_prep",
    )(z, z, z, conv_w, z, alog_row, dtb_row)


def _split3(x):
    hi = x.astype(BF16)
    r1 = x - hi.astype(F32)
    mid = r1.astype(BF16)
    lo = (r1 - mid.astype(F32)).astype(BF16)
    return hi, mid, lo


def _bdot(a, b):
    return jnp.dot(a.astype(BF16), b.astype(BF16), preferred_element_type=F32)


def _gdn_gates(q_ref, k_ref, v_ref, bg_ref, gt_ref, o_ref, d):
    c = GDN_CHUNK
    row = lax.broadcasted_iota(jnp.int32, (c, c), 0)
    col = lax.broadcasted_iota(jnp.int32, (c, c), 1)
    if d == 0:
        incl = row >= col
        strict = row > col
    else:
        incl = row <= col
        strict = row < col
    incl_t = (row <= col) if d == 0 else (row >= col)
    tri = jnp.where(incl, 1.0, 0.0).astype(BF16)
    tri_t = jnp.where(incl_t, 1.0, 0.0).astype(BF16)
    bg = bg_ref[...]
    g_hi, g_mid, g_lo = _split3(bg)
    gc_all = (jnp.dot(tri, g_hi, preferred_element_type=F32) + jnp.dot(tri, g_mid, preferred_element_type=F32)
              + jnp.dot(tri, g_lo, preferred_element_type=F32))
    t_hi, t_mid, t_lo = _split3(gt_ref[0])
    gr_all = (jnp.dot(t_hi, tri_t, preferred_element_type=F32) + jnp.dot(t_mid, tri_t, preferred_element_type=F32)
              + jnp.dot(t_lo, tri_t, preferred_element_type=F32))
    last = c - 1 if d == 0 else 0
    heads = []
    for h in range(GDN_HEADS):
        vh = d * GDN_HEADS + h
        sl = slice(h * LANE, (h + 1) * LANE)
        gcol = gc_all[:, 2 * GDN_HEADS + vh:2 * GDN_HEADS + vh + 1]
        heads.append(dict(
            vh=vh, sl=sl, o_ref=o_ref, incl=incl, strict=strict,
            qh=q_ref[0, :, sl], kh=k_ref[0, :, sl], vv=v_ref[0, :, sl],
            beta=bg[:, vh:vh + 1],
            gcol=gcol,
            grow=gr_all[vh:vh + 1, :],
            g_last=gcol[last:last + 1, :]))
    return heads


def _gdn_kernel(qf_ref, kf_ref, vf_ref, bgf_ref, gtf_ref, qr_ref, kr_ref, vr_ref, bgr_ref, gtr_ref,
                of_ref, or_ref, s_ref):
    @pl.when(pl.program_id(0) == 0)
    def _():
        s_ref[...] = jnp.zeros(s_ref.shape, F32)

    c = GDN_CHUNK
    hs = (_gdn_gates(qf_ref, kf_ref, vf_ref, bgf_ref, gtf_ref, of_ref, 0)
          + _gdn_gates(qr_ref, kr_ref, vr_ref, bgr_ref, gtr_ref, or_ref, 1))
    row = lax.broadcasted_iota(jnp.int32, (c, c), 0)
    col = lax.broadcasted_iota(jnp.int32, (c, c), 1)
    eye = jnp.where(row == col, 1.0, 0.0)
    nt = (((1,), (1,)), ((), ()))
    for t in hs:
        kb = t['kh'].astype(BF16)
        t['qk_kk'] = lax.dot_general(jnp.concatenate([t['qh'].astype(BF16), kb], axis=0), kb, nt,
                                     preferred_element_type=F32)
    for t in hs:
        decay = jnp.exp(jnp.where(t['incl'], t['gcol'] - t['grow'], -jnp.inf))
        t['qk'] = t['qk_kk'][0:c] * decay
        a = jnp.where(t['strict'], t['qk_kk'][c:2 * c] * t['beta'] * decay, 0.0)
        t['pw'] = a
        t['inv'] = eye - a
    for _ in range(5):
        for t in hs:
            t['pw'] = _bdot(t['pw'], t['pw'])
        for t in hs:
            t['inv'] = t['inv'] + _bdot(t['inv'], t['pw'])
    for t in hs:
        eg = jnp.exp(t['gcol'])
        rhs = jnp.concatenate([t['vv'] * t['beta'], t['kh'] * (t['beta'] * eg)], axis=1)
        t['uw'] = _bdot(t['inv'], rhs)
        t['qd'] = t['qh'] * eg
    for t in hs:
        t['s_old'] = s_ref[t['vh']]
        t['ws_qs'] = _bdot(jnp.concatenate([t['uw'][:, LANE:2 * LANE], t['qd']], axis=0), t['s_old'])
    for t in hs:
        t['v_new'] = t['uw'][:, 0:LANE] - t['ws_qs'][0:c]
        t['o_ref'][:, t['sl']] = t['ws_qs'][c:2 * c] + _bdot(t['qk'], t['v_new'])
    for t in hs:
        kt = t['kh'] * jnp.exp(t['g_last'] - t['gcol'])
        s_ref[t['vh']] = t['s_old'] * jnp.exp(t['g_last']) + _bdot(kt.T, t['v_new'])


def gdn_call(qkv, bg, gt, n_lat):
    t = qkv.shape[1]
    c = GDN_CHUNK
    n = t // c
    n_ctx = (t - n_lat) // c

    def fwd(s):
        return (s + n - n_ctx) % n

    def rev(s):
        return n - 1 - s

    def specs(order):
        return [pl.BlockSpec((1, c, GDN_W), lambda s: (0, order(s), 0)),
                pl.BlockSpec((1, c, GDN_W), lambda s: (1, order(s), 0)),
                pl.BlockSpec((1, c, GDN_W), lambda s: (2, order(s), 0)),
                pl.BlockSpec((c, LANE), lambda s: (order(s), 0)),
                pl.BlockSpec((1, 2 * GDN_HEADS, c), lambda s: (order(s), 0, 0))]

    return pl.pallas_call(
        _gdn_kernel,
        out_shape=(jax.ShapeDtypeStruct((t, GDN_W), F32), jax.ShapeDtypeStruct((t, GDN_W), F32)),
        grid=(n,),
        in_specs=specs(fwd) + specs(rev),
        out_specs=(pl.BlockSpec((c, GDN_W), lambda s: (fwd(s), 0)), pl.BlockSpec((c, GDN_W), lambda s: (rev(s), 0))),
        scratch_shapes=[pltpu.VMEM((2 * GDN_HEADS, GDN_DK, GDN_DV), F32)],
        compiler_params=_cparams(("arbitrary",)),
        name="gdn_scan",
    )(qkv, qkv, qkv, bg, gt, qkv, qkv, qkv, bg, gt)


def _rg_direction(x_ref, p_ref, n_ref, cw_ref, cb_ref, wa_ref, ba_ref, wx_ref, bx_ref, lam_ref, o_ref,
                  h_ref, ext_ref, tile, n_lat_tiles, d):
    t = ROW_TILE
    has_prev, has_next = _halo_flags(tile, n_lat_tiles)
    x = _conv4(x_ref, p_ref, n_ref, cw_ref[...], ext_ref, has_prev, has_next) + cb_ref[...]
    xb = x.astype(BF16)
    r_parts, i_parts = [], []
    for n in range(RG_BLOCKS):
        seg = xb[:, n * RG_BS:(n + 1) * RG_BS]
        r_parts.append(jnp.dot(seg, wa_ref[0, n].astype(BF16), preferred_element_type=F32))
        i_parts.append(jnp.dot(seg, wx_ref[0, n].astype(BF16), preferred_element_type=F32))
    r = _sigmoid(jnp.concatenate(r_parts, axis=1) + ba_ref[0])
    ig = _sigmoid(jnp.concatenate(i_parts, axis=1) + bx_ref[0])
    log_a = -RG_C * r * _softplus(-lam_ref[0])
    a = jnp.exp(log_a)
    u = jnp.sqrt(_one_minus_exp(2.0 * log_a)) * (ig * x)
    rows = lax.broadcasted_iota(jnp.int32, (t, 1), 0)
    sh = 1
    while sh < t:
        if d == 0:
            a_s = pltpu.roll(a, sh, 0)
            u_s = pltpu.roll(u, sh, 0)
            ok = rows >= sh
        else:
            a_s = pltpu.roll(a, t - sh, 0)
            u_s = pltpu.roll(u, t - sh, 0)
            ok = rows < t - sh
        u = jnp.where(ok, a * u_s + u, u)
        a = jnp.where(ok, a * a_s, a)
        sh *= 2
    h = u + a * h_ref[d:d + 1, :]
    o_ref[...] = h
    edge = t - 1 if d == 0 else 0
    h_ref[d:d + 1, :] = h[edge:edge + 1, :]


def _rg_kernel(xf_ref, pf_ref, nf_ref, xr_ref, pr_ref, nr_ref, cw_ref, cb_ref,
               waf_ref, baf_ref, wxf_ref, bxf_ref, lamf_ref, war_ref, bar_ref, wxr_ref, bxr_ref, lamr_ref,
               of_ref, or_ref, h_ref, ext_ref, *, n_tiles, n_lat_tiles):
    s = pl.program_id(0)

    @pl.when(s == 0)
    def _():
        h_ref[...] = jnp.zeros(h_ref.shape, F32)

    tile_f = (s + n_lat_tiles) % n_tiles
    tile_r = n_tiles - 1 - s
    _rg_direction(xf_ref, pf_ref, nf_ref, cw_ref, cb_ref, waf_ref, baf_ref, wxf_ref, bxf_ref, lamf_ref, of_ref,
                  h_ref, ext_ref, tile_f, n_lat_tiles, 0)
    _rg_direction(xr_ref, pr_ref, nr_ref, cw_ref, cb_ref, war_ref, bar_ref, wxr_ref, bxr_ref, lamr_ref, or_ref,
                  h_ref, ext_ref, tile_r, n_lat_tiles, 1)


def rg_call(z, conv_w, conv_b, w_a, b_a, w_x, b_x, lam, n_lat):
    t = z.shape[0]
    n_tiles = t // ROW_TILE
    n_lat_tiles = n_lat // ROW_TILE
    r8 = ROW_TILE // 8
    last8 = n_tiles * r8 - 1
    colb = Z_RX // RG_WIDTH

    def fwd(s):
        return (s + n_lat_tiles) % n_tiles

    def rev(s):
        return n_tiles - 1 - s

    def tile_specs(order):
        return [pl.BlockSpec((ROW_TILE, RG_WIDTH), lambda s: (order(s), colb)),
                pl.BlockSpec((8, RG_WIDTH), lambda s: (jnp.maximum(order(s) * r8 - 1, 0), colb)),
                pl.BlockSpec((8, RG_WIDTH), lambda s: (jnp.minimum((order(s) + 1) * r8, last8), colb))]

    def dir_specs(d):
        return [pl.BlockSpec((1, RG_BLOCKS, RG_BS, RG_BS), lambda s: (d, 0, 0, 0)),
                pl.BlockSpec((1, 1, RG_WIDTH), lambda s: (d, 0, 0)),
                pl.BlockSpec((1, RG_BLOCKS, RG_BS, RG_BS), lambda s: (d, 0, 0, 0)),
                pl.BlockSpec((1, 1, RG_WIDTH), lambda s: (d, 0, 0)),
                pl.BlockSpec((1, 1, RG_WIDTH), lambda s: (d, 0, 0))]

    b_a3 = b_a.reshape(2, 1, RG_WIDTH)
    b_x3 = b_x.reshape(2, 1, RG_WIDTH)
    lam3 = lam.reshape(2, 1, RG_WIDTH)
    return pl.pallas_call(
        functools.partial(_rg_kernel, n_tiles=n_tiles, n_lat_tiles=n_lat_tiles),
        out_shape=(jax.ShapeDtypeStruct((t, RG_WIDTH), F32), jax.ShapeDtypeStruct((t, RG_WIDTH), F32)),
        grid=(n_tiles,),
        in_specs=tile_specs(fwd) + tile_specs(rev)
        + [pl.BlockSpec((4, RG_WIDTH), lambda s: (0, 0)), pl.BlockSpec((1, RG_WIDTH), lambda s: (0, 0))]
        + dir_specs(0) + dir_specs(1),
        out_specs=(pl.BlockSpec((ROW_TILE, RG_WIDTH), lambda s: (fwd(s), 0)),
                   pl.BlockSpec((ROW_TILE, RG_WIDTH), lambda s: (rev(s), 0))),
        scratch_shapes=[pltpu.VMEM((8, RG_WIDTH), F32), pltpu.VMEM((ROW_TILE + 16, RG_WIDTH), F32)],
        compiler_params=_cparams(("arbitrary",)),
        name="rg_lru",
    )(z, z, z, z, z, z, conv_w, conv_b, w_a, b_a3, w_x, b_x3, lam3, w_a, b_a3, w_x, b_x3, lam3)


def _assemble_kernel(of_ref, or_ref, gz_ref, on_ref, hf_ref, hr_ref, ry_ref, og_ref, org_ref):
    o = of_ref[...] + or_ref[...]
    gate = _silu(gz_ref[...])
    for h in range(GDN_HEADS):
        sl = slice(h * LANE, (h + 1) * LANE)
        seg = o[:, sl]
        on = seg * lax.rsqrt(jnp.mean(seg * seg, axis=-1, keepdims=True) + EPS) * on_ref[...]
        og_ref[:, sl] = (on * gate[:, sl]).astype(BF16)
    org_ref[...] = ((hf_ref[...] + hr_ref[...]) * _gelu_tanh(ry_ref[...])).astype(BF16)


def assemble_call(of, orv, z, o_norm, hf, hr, m):
    row = pl.BlockSpec((ROW_TILE, 1024), lambda i: (i, 0))
    return pl.pallas_call(
        _assemble_kernel,
        out_shape=(jax.ShapeDtypeStruct((m, GDN_W), BF16), jax.ShapeDtypeStruct((m, RG_WIDTH), BF16)),
        grid=(m // ROW_TILE,),
        in_specs=[row, row, pl.BlockSpec((ROW_TILE, GDN_W), lambda i: (i, Z_GZ // GDN_W)),
                  pl.BlockSpec((1, LANE), lambda i: (0, 0)), row, row,
                  pl.BlockSpec((ROW_TILE, RG_WIDTH), lambda i: (i, Z_RY // RG_WIDTH))],
        out_specs=(row, row),
        compiler_params=_cparams(("arbitrary",)),
        name="mix_assemble",
    )(of, orv, z, o_norm, hf, hr, z)


def _post_mix_kernel(y_ref, x_ref, gpost_ref, gm_ref, gpre_ref, sc_ref, sh_ref, wr_ref,
                     x1_ref, h2_ref, aff_ref, *, ctx_tile):
    is_ctx = pl.program_id(0) >= ctx_tile

    def pick(ref):
        return jnp.where(is_ctx, ref[1:2, :], ref[0:1, :])

    x1 = x_ref[...] + pick(gm_ref) * _rms(y_ref[...], gpost_ref[...])
    x1_ref[...] = x1
    h2 = _rms(x1, gpre_ref[...]) * (1.0 + pick(sc_ref)) + pick(sh_ref)
    h2b = h2.astype(BF16)
    h2r = h2b.astype(F32)
    h2_ref[...] = _pack_bf16_pair(h2r[:, 0:D_MODEL // 2], h2r[:, D_MODEL // 2:])
    h_lo = (h2 - h2b.astype(F32)).astype(BF16)
    wr = wr_ref[...]
    w_hi = wr.astype(BF16)
    w_lo = (wr - w_hi.astype(F32)).astype(BF16)
    nt = (((1,), (1,)), ((), ()))
    logits = (lax.dot_general(w_hi, h2b, nt, preferred_element_type=F32)
              + lax.dot_general(w_hi, h_lo, nt, preferred_element_type=F32)
              + lax.dot_general(w_lo, h2b, nt, preferred_element_type=F32))
    e = jnp.exp(logits - jnp.max(logits, axis=0, keepdims=True))
    aff_ref[...] = e / jnp.sum(e, axis=0, keepdims=True)


def post_mix_call(y, x, gpost, gm, gpre, sc, sh, wr_t, n_lat):
    m = y.shape[0]
    row = pl.BlockSpec((ROW_TILE, D_MODEL), lambda i: (i, 0))
    vec1 = pl.BlockSpec((1, D_MODEL), lambda i: (0, 0))
    vec2 = pl.BlockSpec((2, D_MODEL), lambda i: (0, 0))
    return pl.pallas_call(
        functools.partial(_post_mix_kernel, ctx_tile=n_lat // ROW_TILE),
        out_shape=(jax.ShapeDtypeStruct((m, D_MODEL), F32), jax.ShapeDtypeStruct((m, D_MODEL // 2), jnp.uint32),
                   jax.ShapeDtypeStruct((N_EXPERTS, m), F32)),
        grid=(m // ROW_TILE,),
        in_specs=[row, row, vec1, vec2, vec1, vec2, vec2, pl.BlockSpec((N_EXPERTS, D_MODEL), lambda i: (0, 0))],
        out_specs=(row, pl.BlockSpec((ROW_TILE, D_MODEL // 2), lambda i: (i, 0)),
                   pl.BlockSpec((N_EXPERTS, ROW_TILE), lambda i: (0, i))),
        compiler_params=_cparams(("arbitrary",)),
        name="post_mix",
    )(y, x, gpost, gm, gpre, sc, sh, wr_t)


def _post_ffn_kernel(y_ref, x_ref, g_ref, gf_ref, o_ref, *, ctx_tile):
    is_ctx = pl.program_id(0) >= ctx_tile
    gf = jnp.where(is_ctx, gf_ref[1:2, :], gf_ref[0:1, :])
    o_ref[...] = x_ref[...] + gf * _rms(y_ref[...], g_ref[...])


def post_ffn_call(y, x, gain, gf, n_lat):
    m = y.shape[0]
    row = pl.BlockSpec((ROW_TILE, D_MODEL), lambda i: (i, 0))
    return pl.pallas_call(
        functools.partial(_post_ffn_kernel, ctx_tile=n_lat // ROW_TILE),
        out_shape=jax.ShapeDtypeStruct((m, D_MODEL), F32),
        grid=(m // ROW_TILE,),
        in_specs=[row, row, pl.BlockSpec((1, D_MODEL), lambda i: (0, 0)), pl.BlockSpec((2, D_MODEL), lambda i: (0, 0))],
        out_specs=row,
        compiler_params=_cparams(("arbitrary",)),
        name="post_ffn",
    )(y, x, gain, gf)


def _pack_bf16_pair(lo, hi):
    lo_bits = lax.bitcast_convert_type(lo, jnp.uint32) >> 16
    hi_bits = lax.bitcast_convert_type(hi, jnp.uint32) & jnp.uint32(0xFFFF0000)
    return lo_bits | hi_bits


def _unpack_bf16_pair(w):
    lo = lax.bitcast_convert_type(w << 16, F32).astype(BF16)
    hi = lax.bitcast_convert_type(w & jnp.uint32(0xFFFF0000), F32).astype(BF16)
    return lo, hi


def _moe_up_kernel(idx_ref, h_hbm, wg_ref, wu_ref, o_ref, xbuf, sem, *, cap):
    e = pl.program_id(0)
    n_e = pl.num_programs(0)
    half = D_MODEL // 2

    def issue(expert, slot):
        def body(s, carry):
            row = idx_ref[expert, s]
            pltpu.make_async_copy(h_hbm.at[pl.ds(row, 1), :], xbuf.at[slot, pl.ds(s, 1), :], sem.at[slot]).start()
            return carry
        lax.fori_loop(0, cap, body, 0)

    @pl.when(pl.program_id(1) == 0)
    def _():
        slot = e % 2

        @pl.when(e == 0)
        def _():
            issue(0, 0)

        @pl.when(e + 1 < n_e)
        def _():
            issue(e + 1, 1 - slot)

        pltpu.make_async_copy(h_hbm.at[pl.ds(0, cap), :], xbuf.at[slot], sem.at[slot]).wait()

    x_lo, x_hi = _unpack_bf16_pair(xbuf[e % 2])

    def proj(w_ref):
        return (jnp.dot(x_lo, w_ref[0, 0:half, :].astype(BF16), preferred_element_type=F32)
                + jnp.dot(x_hi, w_ref[0, half:, :].astype(BF16), preferred_element_type=F32))

    o_ref[0] = (_silu(proj(wg_ref)) * proj(wu_ref)).astype(BF16)


def moe_up_call(idx, h_packed, w_gate, w_up, tf=256):
    e, cap = idx.shape
    d = D_MODEL
    return pl.pallas_call(
        functools.partial(_moe_up_kernel, cap=cap),
        out_shape=jax.ShapeDtypeStruct((e, cap, EXPERT_FF), BF16),
        grid_spec=pltpu.PrefetchScalarGridSpec(
            num_scalar_prefetch=1,
            grid=(e, EXPERT_FF // tf),
            in_specs=[pl.BlockSpec(memory_space=pl.ANY),
                      pl.BlockSpec((1, d, tf), lambda i, f, idx_ref: (i, 0, f)),
                      pl.BlockSpec((1, d, tf), lambda i, f, idx_ref: (i, 0, f))],
            out_specs=pl.BlockSpec((1, cap, tf), lambda i, f, idx_ref: (i, 0, f)),
            scratch_shapes=[pltpu.VMEM((2, cap, d // 2), jnp.uint32), pltpu.SemaphoreType.DMA((2,))]),
        compiler_params=_cparams(("arbitrary", "arbitrary")),
        name="moe_up",
    )(idx, h_packed, w_gate, w_up)


def _moe_scatter_kernel(idx_ref, y_ref, acc_in, acc_hbm, buf, sem_in, sem_out, *, rows):
    del acc_in
    e = pl.program_id(0)
    base = pl.program_id(1) * rows

    def gather(s, carry):
        row = idx_ref[e, base + s]
        pltpu.make_async_copy(acc_hbm.at[pl.ds(row, 1), :], buf.at[pl.ds(s, 1), :], sem_in).start()
        return carry

    def scatter(s, carry):
        row = idx_ref[e, base + s]
        pltpu.make_async_copy(buf.at[pl.ds(s, 1), :], acc_hbm.at[pl.ds(row, 1), :], sem_out).start()
        return carry

    lax.fori_loop(0, rows, gather, 0)
    pltpu.make_async_copy(acc_hbm.at[pl.ds(0, rows), :], buf, sem_in).wait()
    buf[...] = buf[...] + y_ref[0]
    lax.fori_loop(0, rows, scatter, 0)
    pltpu.make_async_copy(buf, acc_hbm.at[pl.ds(0, rows), :], sem_out).wait()


def moe_scatter_call(idx, y_e, m):
    e, cap, d = y_e.shape
    rows = cap // 2
    return pl.pallas_call(
        functools.partial(_moe_scatter_kernel, rows=rows),
        out_shape=jax.ShapeDtypeStruct((m, d), F32),
        grid_spec=pltpu.PrefetchScalarGridSpec(
            num_scalar_prefetch=1,
            grid=(e, 2),
            in_specs=[pl.BlockSpec((1, rows, d), lambda i, hf, idx_ref: (i, hf, 0)),
                      pl.BlockSpec(memory_space=pl.ANY)],
            out_specs=pl.BlockSpec(memory_space=pl.ANY),
            scratch_shapes=[pltpu.VMEM((rows, d), F32), pltpu.SemaphoreType.DMA(()), pltpu.SemaphoreType.DMA(())]),
        input_output_aliases={2: 0},
        compiler_params=_cparams(("arbitrary", "arbitrary")),
        name="moe_scatter",
    )(idx, y_e, jnp.zeros((m, d), F32))


def _moe_down_kernel(h_ref, wd_ref, g_ref, o_ref):
    y = jnp.dot(h_ref[0], wd_ref[0].astype(BF16), preferred_element_type=F32)
    o_ref[0] = y * g_ref[0]


def moe_down_call(hid, w_down, gates, tn=512):
    e, c, f = hid.shape
    d = w_down.shape[2]
    return pl.pallas_call(
        _moe_down_kernel,
        out_shape=jax.ShapeDtypeStruct((e, c, d), F32),
        grid=(e, d // tn),
        in_specs=[pl.BlockSpec((1, c, f), lambda i, n: (i, 0, 0)),
                  pl.BlockSpec((1, f, tn), lambda i, n: (i, 0, n)),
                  pl.BlockSpec((1, c, 1), lambda i, n: (i, 0, 0))],
        out_specs=pl.BlockSpec((1, c, tn), lambda i, n: (i, 0, n)),
        compiler_params=_cparams(("arbitrary", "arbitrary")),
        name="moe_down",
    )(hid, w_down, gates)


def _rope_tables(n_lat, n_ctx):
    rows = n_lat // GRID_W
    row = jnp.broadcast_to(jnp.arange(rows, dtype=F32)[:, None], (rows, GRID_W)).reshape(-1)
    col = jnp.broadcast_to(jnp.arange(GRID_W, dtype=F32)[None, :], (rows, GRID_W)).reshape(-1)
    n_freq = MLA_ROPE // 4
    inv_freq = ROPE_THETA ** (-jnp.arange(n_freq, dtype=F32) / n_freq)
    ang = jnp.concatenate([row[:, None] * inv_freq, col[:, None] * inv_freq], axis=-1)
    cos = jnp.cos(ang)
    sin = jnp.sin(ang)
    cos64 = jnp.concatenate([cos, cos], axis=-1)
    sin64 = jnp.concatenate([sin, sin], axis=-1)
    one_l = jnp.ones((n_lat, 64), F32)
    zero_l = jnp.zeros((n_lat, 64), F32)
    one_c = jnp.ones((n_ctx, 64), F32)
    zero_c = jnp.zeros((n_ctx, 64), F32)
    cq = jnp.concatenate([jnp.concatenate([cos64, one_l], 1), jnp.concatenate([one_c, one_c], 1)], 0)
    sq = jnp.concatenate([jnp.concatenate([sin64, zero_l], 1), jnp.concatenate([zero_c, zero_c], 1)], 0)
    ck = jnp.concatenate([jnp.concatenate([cos64, zero_l], 1), jnp.concatenate([zero_c, one_c], 1)], 0)
    sk = jnp.concatenate([jnp.concatenate([sin64, zero_l], 1), jnp.concatenate([zero_c, zero_c], 1)], 0)
    return cq, sq, ck, sk


def _rot_half_cols(w):
    half = MLA_ROPE // 2
    return jnp.concatenate([-w[..., half:], w[..., :half]], axis=-1)


def _relayout_w_in(w_in):
    q_a = w_in[:, 0:1024]
    kv_a = w_in[:, 1024:1536]
    k_pe = w_in[:, 1536:1600]
    g_qkv = w_in[:, 1600:4672]
    g_z = w_in[:, 4672:5696]
    g_ba = w_in[:, 5696:5728]
    r_x = w_in[:, 5728:6752]
    r_y = w_in[:, 6752:7776]
    zeros = functools.partial(jnp.zeros, dtype=w_in.dtype)
    d = w_in.shape[0]
    return jnp.concatenate([q_a, g_qkv, g_z, r_x, r_y, kv_a, k_pe, k_pe, _rot_half_cols(k_pe), zeros((d, 64)),
                            g_ba, zeros((d, LANE - 32)), zeros((d, LANE))], axis=1)


def _relayout_w_qb(w_qb):
    w = w_qb.reshape(MLA_Q_RANK, MLA_HEADS, MLA_NOPE + MLA_ROPE)
    nope = w[:, :, :MLA_NOPE]
    pe = w[:, :, MLA_NOPE:]
    zeros = jnp.zeros_like(pe)
    out = jnp.concatenate([nope, pe, pe, _rot_half_cols(pe), zeros], axis=-1)
    return jnp.transpose(out, (1, 0, 2))


def _layer(x_all, c_rows, p, n_lat, n_ctx, update_ctx):
    t = n_lat + n_ctx
    mod = mod_call(c_rows, p['mod_w'], p['mod_b'][None, :])[0:2]
    sh_m, sc_m, g_m, sh_f, sc_f, g_f = [mod[:, i * D_MODEL:(i + 1) * D_MODEL] for i in range(6)]

    h = prenorm_call(x_all, p['norm_mix_pre'][None, :], sc_m, sh_m, n_lat)
    z = matmul_call(h, _relayout_w_in(p['w_in']), name="w_in")

    cq, sq, ck, sk = _rope_tables(n_lat, n_ctx)
    q = qproj_call(z, p['mla_q_norm'][None, :], _relayout_w_qb(p['mla_w_qb']), cq, sq)
    tk = next(c for c in (768, 256) if t % c == 0)
    k, v = kvproj_call(z, p['mla_kv_norm'][None, :], p['mla_w_kvb'], ck, sk, tk)
    o_mla = attn_call(q, k, v, n_lat, 0, t, 0, tq=512, tk=tk)
    if update_ctx:
        o_mla_c = attn_call(q, k, v, n_ctx, n_lat, n_ctx, n_lat, tq=n_ctx, tk=n_ctx)
        o_mla = jnp.concatenate([o_mla, o_mla_c], axis=0)

    pad = jnp.zeros((2 * GDN_HEADS,), F32)
    alog_row = jnp.concatenate([pad, p['gdn_a_log'].reshape(-1), jnp.zeros((LANE - 4 * GDN_HEADS,), F32)])[None, :]
    dtb_row = jnp.concatenate([pad, p['gdn_dt_bias'].reshape(-1), jnp.zeros((LANE - 4 * GDN_HEADS,), F32)])[None, :]
    qkv, bg = gdn_prep_call(z, p['gdn_conv_w'], alog_row, dtb_row, n_lat)
    gt = jnp.swapaxes(bg[:, 2 * GDN_HEADS:4 * GDN_HEADS].reshape(t // GDN_CHUNK, GDN_CHUNK, 2 * GDN_HEADS), 1, 2)
    o_f, o_r = gdn_call(qkv, bg, gt, n_lat)

    h_f, h_r = rg_call(z, p['rg_conv_w'], p['rg_conv_b'][None, :], p['rg_w_a'], p['rg_b_a'], p['rg_w_x'],
                       p['rg_b_x'], p['rg_lambda'], n_lat)

    m = t if update_ctx else n_lat
    o_gdn, o_rg = assemble_call(o_f, o_r, z, p['gdn_o_norm'][None, :], h_f, h_r, m)
    y = matmul3_call(o_mla, o_gdn, o_rg, p['w_out'], m)
    x1, h2, aff_t = post_mix_call(y, x_all, p['norm_mix_post'][None, :], g_m, p['norm_ffn_pre'][None, :],
                                  sc_f, sh_f, p['w_router'].T, n_lat)

    cap = CAPACITY_FACTOR * n_lat // N_EXPERTS
    gates, idx = lax.top_k(aff_t[:, :n_lat], cap)
    if update_ctx:
        cap_c = CAPACITY_FACTOR * n_ctx // N_EXPERTS
        gates_c, idx_c = lax.top_k(aff_t[:, n_lat:], cap_c)
        gates = jnp.concatenate([gates, gates_c], axis=1)
        idx = jnp.concatenate([idx, idx_c + n_lat], axis=1)
    hid = moe_up_call(idx, h2, p['w_gate'], p['w_up'])
    y_e = moe_down_call(hid, p['w_down'], gates[..., None])
    y_moe = moe_scatter_call(idx, y_e, m)
    return post_ffn_call(y_moe, x1, p['norm_ffn_post'][None, :], g_f, n_lat)


def kernel(x, c, ctx, c_ctx, mod_w, mod_b, norm_mix_pre, norm_mix_post, norm_ffn_pre, norm_ffn_post,
           w_in, mla_q_norm, mla_kv_norm, mla_w_qb, mla_w_kvb, gdn_conv_w, gdn_a_log, gdn_dt_bias,
           gdn_o_norm, rg_conv_w, rg_conv_b, rg_w_a, rg_b_a, rg_w_x, rg_b_x, rg_lambda, w_out,
           w_router, w_gate, w_up, w_down):
    assert x.shape[0] == 1 and ctx.shape[0] == 1
    n_lat, n_ctx = x.shape[1], ctx.shape[1]
    assert n_ctx == ROW_TILE and n_lat % 512 == 0
    depth = mod_w.shape[0]
    stacked = dict(mod_w=mod_w, mod_b=mod_b, norm_mix_pre=norm_mix_pre, norm_mix_post=norm_mix_post,
                   norm_ffn_pre=norm_ffn_pre, norm_ffn_post=norm_ffn_post, w_in=w_in, mla_q_norm=mla_q_norm,
                   mla_kv_norm=mla_kv_norm, mla_w_qb=mla_w_qb, mla_w_kvb=mla_w_kvb, gdn_conv_w=gdn_conv_w,
                   gdn_a_log=gdn_a_log, gdn_dt_bias=gdn_dt_bias, gdn_o_norm=gdn_o_norm, rg_conv_w=rg_conv_w,
                   rg_conv_b=rg_conv_b, rg_w_a=rg_w_a, rg_b_a=rg_b_a, rg_w_x=rg_w_x, rg_b_x=rg_b_x,
                   rg_lambda=rg_lambda, w_out=w_out, w_router=w_router, w_gate=w_gate, w_up=w_up, w_down=w_down)
    c_rows = jnp.concatenate([c, c_ctx[None, :], jnp.zeros((6, D_MODEL), F32)], axis=0)
    x_all = jnp.concatenate([x[0], ctx[0]], axis=0)
    for l in range(depth):
        p = {k: v[l] for k, v in stacked.items()}
        x_all = _layer(x_all, c_rows, p, n_lat, n_ctx, l < depth - 1)
    return x_all[None]
```

```python
import functools
import math

import jax
import jax.numpy as jnp
from jax import lax
from jax.experimental import pallas as pl
from jax.experimental.pallas import tpu as pltpu

F32 = jnp.float32
BF16 = jnp.bfloat16

D_MODEL = 4096
EPS = 1e-6
GRID_W = 64
ROPE_THETA = 10000.0

MLA_HEADS = 16
MLA_NOPE = 128
MLA_ROPE = 64
MLA_V = 128
MLA_Q_RANK = 1024
MLA_KV_RANK = 512
VT_ROWS = MLA_V + 16

GDN_HEADS = 8
GDN_DK = 128
GDN_DV = 128
GDN_CHUNK = 64
GDN_W = GDN_HEADS * GDN_DK

RG_WIDTH = 1024
RG_BLOCKS = 8
RG_BS = RG_WIDTH // RG_BLOCKS
RG_C = 8.0

N_EXPERTS = 16
EXPERT_FF = 1024
CAPACITY_FACTOR = 2

LANE = 128
ROW_TILE = 256
VMEM_LIMIT = 56 * 1024 * 1024

Z_QA, Z_GQ, Z_GK, Z_GV, Z_GZ, Z_RX, Z_RY = 0, 1024, 2048, 3072, 4096, 5120, 6144
Z_KVA = 7168
Z_PEA, Z_PEB, Z_BA = 7680, 7808, 7936
Z_COLS = 8192


def _cparams(sem, vmem=VMEM_LIMIT):
    return pltpu.CompilerParams(dimension_semantics=sem, vmem_limit_bytes=vmem)


def _sigmoid(x):
    return 1.0 / (1.0 + jnp.exp(-x))


def _silu(x):
    return x * _sigmoid(x)


def _softplus(x):
    return jnp.maximum(x, 0.0) + jnp.log(1.0 + jnp.exp(-jnp.abs(x)))


def _one_minus_exp(y):
    poly = 1.0 + y * (1.0 / 13.0)
    for n in range(12, 1, -1):
        poly = 1.0 + (y * (1.0 / n)) * poly
    return jnp.where(y > -0.5, -(y * poly), 1.0 - jnp.exp(y))


def _gelu_tanh(x):
    return 0.5 * x * (1.0 + jnp.tanh(math.sqrt(2.0 / math.pi) * (x + 0.044715 * (x * x * x))))


def _rms(x, gain):
    return x * lax.rsqrt(jnp.mean(x * x, axis=-1, keepdims=True) + EPS) * gain


def _mod_kernel(c_ref, w_ref, b_ref, o_ref):
    s = _silu(c_ref[...])
    o_ref[...] = jnp.dot(s.astype(BF16), w_ref[0].astype(BF16), preferred_element_type=F32) + b_ref[...]


def mod_call(cc, w, b, l):
    tn = 1024
    n = w.shape[2]
    return pl.pallas_call(
        _mod_kernel,
        out_shape=jax.ShapeDtypeStruct((8, n), F32),
        grid=(n // tn,),
        in_specs=[pl.BlockSpec((8, D_MODEL), lambda j: (0, 0)),
                  pl.BlockSpec((1, D_MODEL, tn), lambda j: (l, 0, j)),
                  pl.BlockSpec((1, tn), lambda j: (0, j))],
        out_specs=pl.BlockSpec((8, tn), lambda j: (0, j)),
        compiler_params=_cparams(("arbitrary",)),
        name="mod",
    )(cc, w, b)


def _prenorm_kernel(x_ref, g_ref, sc_ref, sh_ref, o_ref, *, ctx_tile):
    is_ctx = pl.program_id(0) >= ctx_tile
    sc = jnp.where(is_ctx, sc_ref[1:2, :], sc_ref[0:1, :])
    sh = jnp.where(is_ctx, sh_ref[1:2, :], sh_ref[0:1, :])
    y = _rms(x_ref[...], g_ref[...])
    o_ref[...] = (y * (1.0 + sc) + sh).astype(BF16)


def prenorm_call(x, gain, sc, sh, n_lat):
    t = x.shape[0]
    row = pl.BlockSpec((ROW_TILE, D_MODEL), lambda i: (i, 0))
    return pl.pallas_call(
        functools.partial(_prenorm_kernel, ctx_tile=n_lat // ROW_TILE),
        out_shape=jax.ShapeDtypeStruct((t, D_MODEL), BF16),
        grid=(t // ROW_TILE,),
        in_specs=[row, pl.BlockSpec((1, D_MODEL), lambda i: (0, 0)),
                  pl.BlockSpec((2, D_MODEL), lambda i: (0, 0)), pl.BlockSpec((2, D_MODEL), lambda i: (0, 0))],
        out_specs=row,
        compiler_params=_cparams(("arbitrary",)),
        name="prenorm",
    )(x, gain, sc, sh)


def _matmul_kernel(a_ref, w_ref, o_ref):
    o_ref[...] = jnp.dot(a_ref[...], w_ref[...].astype(BF16), preferred_element_type=F32)


def _pick_tm(m):
    for tm in (1056, 1024, 768, 512, 256):
        if m % tm == 0:
            return tm
    raise ValueError(f"unsupported row count {m}")


def matmul_call(a, w, tn=512, name="matmul"):
    m, k = a.shape
    n = w.shape[1]
    tm = _pick_tm(m)
    return pl.pallas_call(
        _matmul_kernel,
        out_shape=jax.ShapeDtypeStruct((m, n), F32),
        grid=(m // tm, n // tn),
        in_specs=[pl.BlockSpec((tm, k), lambda i, j: (i, 0)), pl.BlockSpec((k, tn), lambda i, j: (0, j))],
        out_specs=pl.BlockSpec((tm, tn), lambda i, j: (i, j)),
        compiler_params=_cparams(("arbitrary", "arbitrary")),
        name=name,
    )(a, w)


def _matmul3_kernel(a1_ref, a2_ref, a3_ref, w_ref, o_ref):
    k1 = a1_ref.shape[1]
    k2 = a2_ref.shape[1]
    acc = jnp.dot(a1_ref[...], w_ref[0, 0:k1, :].astype(BF16), preferred_element_type=F32)
    acc += jnp.dot(a2_ref[...], w_ref[0, k1:k1 + k2, :].astype(BF16), preferred_element_type=F32)
    acc += jnp.dot(a3_ref[...], w_ref[0, k1 + k2:, :].astype(BF16), preferred_element_type=F32)
    o_ref[...] = acc


def matmul3_call(a1, a2, a3, w, l, m, tn=512):
    k = w.shape[1]
    n = w.shape[2]
    tm = _pick_tm(m)
    return pl.pallas_call(
        _matmul3_kernel,
        out_shape=jax.ShapeDtypeStruct((m, n), F32),
        grid=(m // tm, n // tn),
        in_specs=[pl.BlockSpec((tm, a1.shape[1]), lambda i, j: (i, 0)),
                  pl.BlockSpec((tm, a2.shape[1]), lambda i, j: (i, 0)),
                  pl.BlockSpec((tm, a3.shape[1]), lambda i, j: (i, 0)),
                  pl.BlockSpec((1, k, tn), lambda i, j: (l, 0, j))],
        out_specs=pl.BlockSpec((tm, tn), lambda i, j: (i, j)),
        compiler_params=_cparams(("arbitrary", "arbitrary")),
        name="w_out",
    )(a1, a2, a3, w)


def _qproj_kernel(z_ref, g_ref, w_ref, c_ref, s_ref, o_ref, zn_ref, *, scale):
    @pl.when(pl.program_id(1) == 0)
    def _():
        zn_ref[...] = _rms(z_ref[...], g_ref[...]).astype(BF16)

    r = jnp.dot(zn_ref[...], w_ref[0].astype(BF16), preferred_element_type=F32)
    q0 = r[:, 0:LANE]
    q1 = r[:, LANE:2 * LANE] * c_ref[...] + r[:, 2 * LANE:3 * LANE] * s_ref[...]
    o_ref[0, :, 0:LANE] = (q0 * scale).astype(BF16)
    o_ref[0, :, LANE:2 * LANE] = (q1 * scale).astype(BF16)


def qproj_call(z, q_norm, wq, cq, sq):
    t = z.shape[0]
    tm = _pick_tm(t)
    scale = (MLA_NOPE + MLA_ROPE) ** -0.5 * math.log2(math.e)
    return pl.pallas_call(
        functools.partial(_qproj_kernel, scale=scale),
        out_shape=jax.ShapeDtypeStruct((MLA_HEADS, t, 2 * LANE), BF16),
        grid=(t // tm, MLA_HEADS),
        in_specs=[pl.BlockSpec((tm, MLA_Q_RANK), lambda i, h: (i, Z_QA // MLA_Q_RANK)),
                  pl.BlockSpec((1, MLA_Q_RANK), lambda i, h: (0, 0)),
                  pl.BlockSpec((1, MLA_Q_RANK, 3 * LANE), lambda i, h: (h, 0, 0)),
                  pl.BlockSpec((tm, LANE), lambda i, h: (i, 0)),
                  pl.BlockSpec((tm, LANE), lambda i, h: (i, 0))],
        out_specs=pl.BlockSpec((1, tm, 2 * LANE), lambda i, h: (h, i, 0)),
        scratch_shapes=[pltpu.VMEM((tm, MLA_Q_RANK), BF16)],
        compiler_params=_cparams(("arbitrary", "arbitrary")),
        name="q_proj",
    )(z, q_norm, wq, cq, sq)


def _kvproj_kernel(z_ref, g_ref, w_ref, pa_ref, pb_ref, c_ref, s_ref, k_ref, v_ref, zn_ref):
    @pl.when(pl.program_id(1) == 0)
    def _():
        zn_ref[...] = _rms(z_ref[...], g_ref[...]).astype(BF16)

    r = jnp.dot(zn_ref[...], w_ref[0].astype(BF16), preferred_element_type=F32)
    k1 = pa_ref[...] * c_ref[...] + pb_ref[...] * s_ref[...]
    k_ref[0, :, 0:LANE] = r[:, 0:LANE].astype(BF16)
    k_ref[0, :, LANE:2 * LANE] = k1.astype(BF16)
    v_ref[0, 0, 0:MLA_V, :] = r[:, LANE:2 * LANE].T.astype(BF16)
    v_ref[0, 0, MLA_V:VT_ROWS, :] = jnp.ones((VT_ROWS - MLA_V, r.shape[0]), BF16)


def kvproj_call(z, kv_norm, wkv, l, ck, sk, tm):
    t = z.shape[0]
    return pl.pallas_call(
        _kvproj_kernel,
        out_shape=(jax.ShapeDtypeStruct((MLA_HEADS, t, 2 * LANE), BF16),
                   jax.ShapeDtypeStruct((MLA_HEADS, t // tm, VT_ROWS, tm), BF16)),
        grid=(t // tm, MLA_HEADS),
        in_specs=[pl.BlockSpec((tm, MLA_KV_RANK), lambda i, h: (i, Z_KVA // MLA_KV_RANK)),
                  pl.BlockSpec((1, MLA_KV_RANK), lambda i, h: (0, 0)),
                  pl.BlockSpec((1, MLA_KV_RANK, 2 * LANE), lambda i, h: (l, 0, h)),
                  pl.BlockSpec((tm, LANE), lambda i, h: (i, Z_PEA // LANE)),
                  pl.BlockSpec((tm, LANE), lambda i, h: (i, Z_PEB // LANE)),
                  pl.BlockSpec((tm, LANE), lambda i, h: (i, 0)),
                  pl.BlockSpec((tm, LANE), lambda i, h: (i, 0))],
        out_specs=(pl.BlockSpec((1, tm, 2 * LANE), lambda i, h: (h, i, 0)),
                   pl.BlockSpec((1, 1, VT_ROWS, tm), lambda i, h: (h, i, 0, 0))),
        scratch_shapes=[pltpu.VMEM((tm, MLA_KV_RANK), BF16)],
        compiler_params=_cparams(("arbitrary", "arbitrary")),
        name="kv_proj",
    )(z, kv_norm, wkv, z, z, ck, sk)


ATTN_LOOKAHEAD = 2


def _attn_kernel(q_ref, k_ref, vt_ref, o_ref, m_ref, acc_ref, *bufs, tk, n_chunks):
    q = q_ref[0]
    nt = (((1,), (1,)), ((), ()))

    def scores(c):
        return lax.dot_general(k_ref[0, c * tk:(c + 1) * tk, :], q, nt, preferred_element_type=F32)

    def consume(c, s_ref):
        m_old = m_ref[...]
        m_new = jnp.maximum(m_old, jnp.max(s_ref[...], axis=0, keepdims=True))
        alpha = jnp.exp2(m_old - m_new)
        p = jnp.exp2(s_ref[...] - m_new)
        pv = jnp.dot(vt_ref[0, c], p.astype(BF16), preferred_element_type=F32)
        acc_ref[...] = alpha * acc_ref[...] + pv
        m_ref[...] = m_new

    m_ref[...] = jnp.full(m_ref.shape, -jnp.inf, F32)
    acc_ref[...] = jnp.zeros(acc_ref.shape, F32)
    for c in range(min(ATTN_LOOKAHEAD, n_chunks)):
        bufs[c][...] = scores(c)
    for c in range(n_chunks):
        if c + ATTN_LOOKAHEAD < n_chunks:
            bufs[(c + ATTN_LOOKAHEAD) % len(bufs)][...] = scores(c + ATTN_LOOKAHEAD)
        consume(c, bufs[c % len(bufs)])
    o_ref[...] = (acc_ref[0:MLA_V, :] / acc_ref[MLA_V:MLA_V + 1, :]).T.astype(BF16)


def attn_call(q, k, vt, n_q, q_row0, n_k, k_row0, tq, tk):
    ck = vt.shape[3]
    assert n_q % tq == 0 and q_row0 % tq == 0 and n_k % tk == 0 and k_row0 % n_k == 0
    n_chunks = n_k // tk
    if n_chunks > 1:
        assert tk == ck and k_row0 == 0
        vt_spec = pl.BlockSpec((1, n_chunks, VT_ROWS, ck), lambda h, j: (h, 0, 0, 0))
    else:
        assert (k_row0 % ck) % tk == 0
        vt_spec = pl.BlockSpec((1, 1, VT_ROWS, tk), lambda h, j: (h, k_row0 // ck, 0, (k_row0 % ck) // tk))
    qb, kb = q_row0 // tq, k_row0 // n_k
    return pl.pallas_call(
        functools.partial(_attn_kernel, tk=tk, n_chunks=n_chunks),
        out_shape=jax.ShapeDtypeStruct((n_q, MLA_HEADS * MLA_V), BF16),
        grid=(MLA_HEADS, n_q // tq),
        in_specs=[pl.BlockSpec((1, tq, 2 * LANE), lambda h, j: (h, j + qb, 0)),
                  pl.BlockSpec((1, n_k, 2 * LANE), lambda h, j: (h, kb, 0)),
                  vt_spec],
        out_specs=pl.BlockSpec((tq, MLA_V), lambda h, j: (j, h)),
        scratch_shapes=[pltpu.VMEM((1, tq), F32), pltpu.VMEM((VT_ROWS, tq), F32)]
        + [pltpu.VMEM((tk, tq), F32)] * (ATTN_LOOKAHEAD + 1),
        compiler_params=_cparams(("arbitrary", "arbitrary")),
        name="mla_attn",
    )(q, k, vt)


def _conv4(x_ref, p_ref, n_ref, w, ext_ref, has_prev, has_next):
    t = x_ref.shape[0]
    ext_ref[0:8, :] = jnp.where(has_prev, p_ref[...], 0.0)
    ext_ref[8:8 + t, :] = x_ref[...]
    ext_ref[8 + t:16 + t, :] = jnp.where(has_next, n_ref[...], 0.0)
    y = ext_ref[6:6 + t, :] * w[0:1, :]
    y += ext_ref[7:7 + t, :] * w[1:2, :]
    y += ext_ref[8:8 + t, :] * w[2:3, :]
    y += ext_ref[9:9 + t, :] * w[3:4, :]
    return y


def _halo_flags(i, n_lat_tiles):
    has_prev = jnp.logical_and(i > 0, i < n_lat_tiles)
    has_next = i < n_lat_tiles - 1
    return has_prev, has_next


def _halo_specs(width, col_of, n_tiles):
    r8 = ROW_TILE // 8
    last8 = n_tiles * r8 - 1
    main = pl.BlockSpec((ROW_TILE, width), lambda i, *a: (i, col_of(*a)))
    prev = pl.BlockSpec((8, width), lambda i, *a: (jnp.maximum(i * r8 - 1, 0), col_of(*a)))
    nxt = pl.BlockSpec((8, width), lambda i, *a: (jnp.minimum((i + 1) * r8, last8), col_of(*a)))
    return main, prev, nxt


def _gdn_prep_kernel(x_ref, p_ref, n_ref, w_ref, ba_ref, alog_ref, dtb_ref, o_ref, bg_ref, ext_ref, *, n_lat_tiles):
    i = pl.program_id(0)
    j = pl.program_id(1)
    has_prev, has_next = _halo_flags(i, n_lat_tiles)
    y = _silu(_conv4(x_ref, p_ref, n_ref, w_ref[...], ext_ref, has_prev, has_next))
    qscale = jnp.where(j == 0, GDN_DK ** -0.5, 1.0)
    is_v = j == 2
    for h in range(GDN_HEADS):
        seg = y[:, h * LANE:(h + 1) * LANE]
        nrm = seg * lax.rsqrt(jnp.sum(seg * seg, axis=-1, keepdims=True) + EPS) * qscale
        o_ref[0, :, h * LANE:(h + 1) * LANE] = jnp.where(is_v, seg, nrm)

    @pl.when(j == 0)
    def _():
        ba = ba_ref[...]
        lane = lax.broadcasted_iota(jnp.int32, ba.shape, 1)
        beta = _sigmoid(ba)
        g = -jnp.exp(alog_ref[...]) * _softplus(ba + dtb_ref[...])
        bg_ref[...] = jnp.where(lane < 2 * GDN_HEADS, beta, g)


def gdn_prep_call(z, conv_w, alog_row, dtb_row, n_lat):
    t = z.shape[0]
    n_tiles = t // ROW_TILE
    main, prev, nxt = _halo_specs(GDN_W, lambda j: Z_GQ // GDN_W + j, n_tiles)
    row128 = pl.BlockSpec((1, LANE), lambda i, j: (0, 0))
    return pl.pallas_call(
        functools.partial(_gdn_prep_kernel, n_lat_tiles=n_lat // ROW_TILE),
        out_shape=(jax.ShapeDtypeStruct((3, t, GDN_W), F32), jax.ShapeDtypeStruct((t, LANE), F32)),
        grid=(n_tiles, 3),
        in_specs=[main, prev, nxt,
                  pl.BlockSpec((4, GDN_W), lambda i, j: (0, j)),
                  pl.BlockSpec((ROW_TILE, LANE), lambda i, j: (i, Z_BA // LANE)),
                  row128, row128],
        out_specs=(pl.BlockSpec((1, ROW_TILE, GDN_W), lambda i, j: (j, i, 0)),
                   pl.BlockSpec((ROW_TILE, LANE), lambda i, j: (i, 0))),
        scratch_shapes=[pltpu.VMEM((ROW_TILE + 16, GDN_W), F32)],
        compiler_params=_cparams(("arbitrary", "arbitrary")),
        name="gdn_prep",
    )(z, z, z, conv_w, z, alog_row, dtb_row)


def _split3(x):
    hi = x.astype(BF16)
    r1 = x - hi.astype(F32)
    mid = r1.astype(BF16)
    lo = (r1 - mid.astype(F32)).astype(BF16)
    return hi, mid, lo


def _bdot(a, b):
    return jnp.dot(a.astype(BF16), b.astype(BF16), preferred_element_type=F32)


def _gdn_gates(q_ref, k_ref, v_ref, bg_ref, gt_ref, o_ref, d):
    c = GDN_CHUNK
    row = lax.broadcasted_iota(jnp.int32, (c, c), 0)
    col = lax.broadcasted_iota(jnp.int32, (c, c), 1)
    if d == 0:
        incl = row >= col
        strict = row > col
    else:
        incl = row <= col
        strict = row < col
    incl_t = (row <= col) if d == 0 else (row >= col)
    tri = jnp.where(incl, 1.0, 0.0).astype(BF16)
    tri_t = jnp.where(incl_t, 1.0, 0.0).astype(BF16)
    bg = bg_ref[...]
    g_hi, g_mid, g_lo = _split3(bg)
    gc_all = (jnp.dot(tri, g_hi, preferred_element_type=F32) + jnp.dot(tri, g_mid, preferred_element_type=F32)
              + jnp.dot(tri, g_lo, preferred_element_type=F32))
    t_hi, t_mid, t_lo = _split3(gt_ref[0])
    gr_all = (jnp.dot(t_hi, tri_t, preferred_element_type=F32) + jnp.dot(t_mid, tri_t, preferred_element_type=F32)
              + jnp.dot(t_lo, tri_t, preferred_element_type=F32))
    last = c - 1 if d == 0 else 0
    heads = []
    for h in range(GDN_HEADS):
        vh = d * GDN_HEADS + h
        sl = slice(h * LANE, (h + 1) * LANE)
        gcol = gc_all[:, 2 * GDN_HEADS + vh:2 * GDN_HEADS + vh + 1]
        heads.append(dict(
            vh=vh, sl=sl, o_ref=o_ref, incl=incl, strict=strict,
            qh=q_ref[0, :, sl], kh=k_ref[0, :, sl], vv=v_ref[0, :, sl],
            beta=bg[:, vh:vh + 1],
            gcol=gcol,
            grow=gr_all[vh:vh + 1, :],
            g_last=gcol[last:last + 1, :]))
    return heads


def _gdn_kernel(qf_ref, kf_ref, vf_ref, bgf_ref, gtf_ref, qr_ref, kr_ref, vr_ref, bgr_ref, gtr_ref,
                of_ref, or_ref, s_ref):
    @pl.when(pl.program_id(0) == 0)
    def _():
        s_ref[...] = jnp.zeros(s_ref.shape, F32)

    c = GDN_CHUNK
    hs = (_gdn_gates(qf_ref, kf_ref, vf_ref, bgf_ref, gtf_ref, of_ref, 0)
          + _gdn_gates(qr_ref, kr_ref, vr_ref, bgr_ref, gtr_ref, or_ref, 1))
    row = lax.broadcasted_iota(jnp.int32, (c, c), 0)
    col = lax.broadcasted_iota(jnp.int32, (c, c), 1)
    eye = jnp.where(row == col, 1.0, 0.0)
    nt = (((1,), (1,)), ((), ()))
    for t in hs:
        kb = t['kh'].astype(BF16)
        t['qk_kk'] = lax.dot_general(jnp.concatenate([t['qh'].astype(BF16), kb], axis=0), kb, nt,
                                     preferred_element_type=F32)
    for t in hs:
        decay = jnp.exp(jnp.where(t['incl'], t['gcol'] - t['grow'], -jnp.inf))
        t['qk'] = t['qk_kk'][0:c] * decay
        a = jnp.where(t['strict'], t['qk_kk'][c:2 * c] * t['beta'] * decay, 0.0)
        t['pw'] = a
        t['inv'] = eye - a
    for _ in range(5):
        for t in hs:
            t['pw'] = _bdot(t['pw'], t['pw'])
        for t in hs:
            t['inv'] = t['inv'] + _bdot(t['inv'], t['pw'])
    for t in hs:
        eg = jnp.exp(t['gcol'])
        rhs = jnp.concatenate([t['vv'] * t['beta'], t['kh'] * (t['beta'] * eg)], axis=1)
        t['uw'] = _bdot(t['inv'], rhs)
        t['qd'] = t['qh'] * eg
    for t in hs:
        t['s_old'] = s_ref[t['vh']]
        t['ws_qs'] = _bdot(jnp.concatenate([t['uw'][:, LANE:2 * LANE], t['qd']], axis=0), t['s_old'])
    for t in hs:
        t['v_new'] = t['uw'][:, 0:LANE] - t['ws_qs'][0:c]
        t['o_ref'][:, t['sl']] = t['ws_qs'][c:2 * c] + _bdot(t['qk'], t['v_new'])
    for t in hs:
        kt = t['kh'] * jnp.exp(t['g_last'] - t['gcol'])
        s_ref[t['vh']] = t['s_old'] * jnp.exp(t['g_last']) + _bdot(kt.T, t['v_new'])


def gdn_call(qkv, bg, gt, n_lat):
    t = qkv.shape[1]
    c = GDN_CHUNK
    n = t // c
    n_ctx = (t - n_lat) // c

    def fwd(s):
        return (s + n - n_ctx) % n

    def rev(s):
        return n - 1 - s

    def specs(order):
        return [pl.BlockSpec((1, c, GDN_W), lambda s: (0, order(s), 0)),
                pl.BlockSpec((1, c, GDN_W), lambda s: (1, order(s), 0)),
                pl.BlockSpec((1, c, GDN_W), lambda s: (2, order(s), 0)),
                pl.BlockSpec((c, LANE), lambda s: (order(s), 0)),
                pl.BlockSpec((1, 2 * GDN_HEADS, c), lambda s: (order(s), 0, 0))]

    return pl.pallas_call(
        _gdn_kernel,
        out_shape=(jax.ShapeDtypeStruct((t, GDN_W), F32), jax.ShapeDtypeStruct((t, GDN_W), F32)),
        grid=(n,),
        in_specs=specs(fwd) + specs(rev),
        out_specs=(pl.BlockSpec((c, GDN_W), lambda s: (fwd(s), 0)), pl.BlockSpec((c, GDN_W), lambda s: (rev(s), 0))),
        scratch_shapes=[pltpu.VMEM((2 * GDN_HEADS, GDN_DK, GDN_DV), F32)],
        compiler_params=_cparams(("arbitrary",)),
        name="gdn_scan",
    )(qkv, qkv, qkv, bg, gt, qkv, qkv, qkv, bg, gt)


def _rg_direction(x_ref, p_ref, n_ref, cw_ref, cb_ref, wa_ref, ba_ref, wx_ref, bx_ref, lam_ref, o_ref,
                  h_ref, ext_ref, tile, n_lat_tiles, d):
    t = ROW_TILE
    has_prev, has_next = _halo_flags(tile, n_lat_tiles)
    x = _conv4(x_ref, p_ref, n_ref, cw_ref[...], ext_ref, has_prev, has_next) + cb_ref[...]
    xb = x.astype(BF16)
    r_parts, i_parts = [], []
    for n in range(RG_BLOCKS):
        seg = xb[:, n * RG_BS:(n + 1) * RG_BS]
        r_parts.append(jnp.dot(seg, wa_ref[0, n].astype(BF16), preferred_element_type=F32))
        i_parts.append(jnp.dot(seg, wx_ref[0, n].astype(BF16), preferred_element_type=F32))
    r = _sigmoid(jnp.concatenate(r_parts, axis=1) + ba_ref[0])
    ig = _sigmoid(jnp.concatenate(i_parts, axis=1) + bx_ref[0])
    log_a = -RG_C * r * _softplus(-lam_ref[0])
    a = jnp.exp(log_a)
    u = jnp.sqrt(_one_minus_exp(2.0 * log_a)) * (ig * x)
    rows = lax.broadcasted_iota(jnp.int32, (t, 1), 0)
    sh = 1
    while sh < t:
        if d == 0:
            a_s = pltpu.roll(a, sh, 0)
            u_s = pltpu.roll(u, sh, 0)
            ok = rows >= sh
        else:
            a_s = pltpu.roll(a, t - sh, 0)
            u_s = pltpu.roll(u, t - sh, 0)
            ok = rows < t - sh
        u = jnp.where(ok, a * u_s + u, u)
        a = jnp.where(ok, a * a_s, a)
        sh *= 2
    h = u + a * h_ref[d:d + 1, :]
    o_ref[...] = h
    edge = t - 1 if d == 0 else 0
    h_ref[d:d + 1, :] = h[edge:edge + 1, :]


def _rg_kernel(xf_ref, pf_ref, nf_ref, xr_ref, pr_ref, nr_ref, cw_ref, cb_ref,
               waf_ref, baf_ref, wxf_ref, bxf_ref, lamf_ref, war_ref, bar_ref, wxr_ref, bxr_ref, lamr_ref,
               of_ref, or_ref, h_ref, ext_ref, *, n_tiles, n_lat_tiles):
    s = pl.program_id(0)

    @pl.when(s == 0)
    def _():
        h_ref[...] = jnp.zeros(h_ref.shape, F32)

    tile_f = (s + n_lat_tiles) % n_tiles
    tile_r = n_tiles - 1 - s
    _rg_direction(xf_ref, pf_ref, nf_ref, cw_ref, cb_ref, waf_ref, baf_ref, wxf_ref, bxf_ref, lamf_ref, of_ref,
                  h_ref, ext_ref, tile_f, n_lat_tiles, 0)
    _rg_direction(xr_ref, pr_ref, nr_ref, cw_ref, cb_ref, war_ref, bar_ref, wxr_ref, bxr_ref, lamr_ref, or_ref,
                  h_ref, ext_ref, tile_r, n_lat_tiles, 1)


def rg_call(z, conv_w, conv_b, w_a, b_a, w_x, b_x, lam, n_lat):
    t = z.shape[0]
    n_tiles = t // ROW_TILE
    n_lat_tiles = n_lat // ROW_TILE
    r8 = ROW_TILE // 8
    last8 = n_tiles * r8 - 1
    colb = Z_RX // RG_WIDTH

    def fwd(s):
        return (s + n_lat_tiles) % n_tiles

    def rev(s):
        return n_tiles - 1 - s

    def tile_specs(order):
        return [pl.BlockSpec((ROW_TILE, RG_WIDTH), lambda s: (order(s), colb)),
                pl.BlockSpec((8, RG_WIDTH), lambda s: (jnp.maximum(order(s) * r8 - 1, 0), colb)),
                pl.BlockSpec((8, RG_WIDTH), lambda s: (jnp.minimum((order(s) + 1) * r8, last8), colb))]

    def dir_specs(d):
        return [pl.BlockSpec((1, RG_BLOCKS, RG_BS, RG_BS), lambda s: (d, 0, 0, 0)),
                pl.BlockSpec((1, 1, RG_WIDTH), lambda s: (d, 0, 0)),
                pl.BlockSpec((1, RG_BLOCKS, RG_BS, RG_BS), lambda s: (d, 0, 0, 0)),
                pl.BlockSpec((1, 1, RG_WIDTH), lambda s: (d, 0, 0)),
                pl.BlockSpec((1, 1, RG_WIDTH), lambda s: (d, 0, 0))]

    b_a3 = b_a.reshape(2, 1, RG_WIDTH)
    b_x3 = b_x.reshape(2, 1, RG_WIDTH)
    lam3 = lam.reshape(2, 1, RG_WIDTH)
    return pl.pallas_call(
        functools.partial(_rg_kernel, n_tiles=n_tiles, n_lat_tiles=n_lat_tiles),
        out_shape=(jax.ShapeDtypeStruct((t, RG_WIDTH), F32), jax.ShapeDtypeStruct((t, RG_WIDTH), F32)),
        grid=(n_tiles,),
        in_specs=tile_specs(fwd) + tile_specs(rev)
        + [pl.BlockSpec((4, RG_WIDTH), lambda s: (0, 0)), pl.BlockSpec((1, RG_WIDTH), lambda s: (0, 0))]
        + dir_specs(0) + dir_specs(1),
        out_specs=(pl.BlockSpec((ROW_TILE, RG_WIDTH), lambda s: (fwd(s), 0)),
                   pl.BlockSpec((ROW_TILE, RG_WIDTH), lambda s: (rev(s), 0))),
        scratch_shapes=[pltpu.VMEM((8, RG_WIDTH), F32), pltpu.VMEM((ROW_TILE + 16, RG_WIDTH), F32)],
        compiler_params=_cparams(("arbitrary",)),
        name="rg_lru",
    )(z, z, z, z, z, z, conv_w, conv_b, w_a, b_a3, w_x, b_x3, lam3, w_a, b_a3, w_x, b_x3, lam3)


def _assemble_kernel(of_ref, or_ref, gz_ref, on_ref, hf_ref, hr_ref, ry_ref, og_ref, org_ref):
    o = of_ref[...] + or_ref[...]
    gate = _silu(gz_ref[...])
    for h in range(GDN_HEADS):
        sl = slice(h * LANE, (h + 1) * LANE)
        seg = o[:, sl]
        on = seg * lax.rsqrt(jnp.mean(seg * seg, axis=-1, keepdims=True) + EPS) * on_ref[...]
        og_ref[:, sl] = (on * gate[:, sl]).astype(BF16)
    org_ref[...] = ((hf_ref[...] + hr_ref[...]) * _gelu_tanh(ry_ref[...])).astype(BF16)


def assemble_call(of, orv, z, o_norm, hf, hr, m):
    row = pl.BlockSpec((ROW_TILE, 1024), lambda i: (i, 0))
    return pl.pallas_call(
        _assemble_kernel,
        out_shape=(jax.ShapeDtypeStruct((m, GDN_W), BF16), jax.ShapeDtypeStruct((m, RG_WIDTH), BF16)),
        grid=(m // ROW_TILE,),
        in_specs=[row, row, pl.BlockSpec((ROW_TILE, GDN_W), lambda i: (i, Z_GZ // GDN_W)),
                  pl.BlockSpec((1, LANE), lambda i: (0, 0)), row, row,
                  pl.BlockSpec((ROW_TILE, RG_WIDTH), lambda i: (i, Z_RY // RG_WIDTH))],
        out_specs=(row, row),
        compiler_params=_cparams(("arbitrary",)),
        name="mix_assemble",
    )(of, orv, z, o_norm, hf, hr, z)


def _post_mix_kernel(y_ref, x_ref, gpost_ref, gm_ref, gpre_ref, sc_ref, sh_ref, wr_ref,
                     x1_ref, h2_ref, aff_ref, *, ctx_tile):
    is_ctx = pl.program_id(0) >= ctx_tile

    def pick(ref):
        return jnp.where(is_ctx, ref[1:2, :], ref[0:1, :])

    x1 = x_ref[...] + pick(gm_ref) * _rms(y_ref[...], gpost_ref[...])
    x1_ref[...] = x1
    h2 = _rms(x1, gpre_ref[...]) * (1.0 + pick(sc_ref)) + pick(sh_ref)
    h2b = h2.astype(BF16)
    h2r = h2b.astype(F32)
    h2_ref[...] = _pack_bf16_pair(h2r[:, 0:D_MODEL // 2], h2r[:, D_MODEL // 2:])
    h_lo = (h2 - h2b.astype(F32)).astype(BF16)
    wr = wr_ref[...]
    w_hi = wr.astype(BF16)
    w_lo = (wr - w_hi.astype(F32)).astype(BF16)
    nt = (((1,), (1,)), ((), ()))
    logits = (lax.dot_general(w_hi, h2b, nt, preferred_element_type=F32)
              + lax.dot_general(w_hi, h_lo, nt, preferred_element_type=F32)
              + lax.dot_general(w_lo, h2b, nt, preferred_element_type=F32))
    e = jnp.exp(logits - jnp.max(logits, axis=0, keepdims=True))
    aff_ref[...] = e / jnp.sum(e, axis=0, keepdims=True)


def post_mix_call(y, x, gpost, gm, gpre, sc, sh, wr_t, n_lat):
    m = y.shape[0]
    row = pl.BlockSpec((ROW_TILE, D_MODEL), lambda i: (i, 0))
    vec1 = pl.BlockSpec((1, D_MODEL), lambda i: (0, 0))
    vec2 = pl.BlockSpec((2, D_MODEL), lambda i: (0, 0))
    return pl.pallas_call(
        functools.partial(_post_mix_kernel, ctx_tile=n_lat // ROW_TILE),
        out_shape=(jax.ShapeDtypeStruct((m, D_MODEL), F32), jax.ShapeDtypeStruct((m, D_MODEL // 2), jnp.uint32),
                   jax.ShapeDtypeStruct((N_EXPERTS, m), F32)),
        grid=(m // ROW_TILE,),
        in_specs=[row, row, vec1, vec2, vec1, vec2, vec2, pl.BlockSpec((N_EXPERTS, D_MODEL), lambda i: (0, 0))],
        out_specs=(row, pl.BlockSpec((ROW_TILE, D_MODEL // 2), lambda i: (i, 0)),
                   pl.BlockSpec((N_EXPERTS, ROW_TILE), lambda i: (0, i))),
        compiler_params=_cparams(("arbitrary",)),
        name="post_mix",
    )(y, x, gpost, gm, gpre, sc, sh, wr_t)


def _post_ffn_kernel(y_ref, x_ref, g_ref, gf_ref, o_ref, *, ctx_tile):
    is_ctx = pl.program_id(0) >= ctx_tile
    gf = jnp.where(is_ctx, gf_ref[1:2, :], gf_ref[0:1, :])
    o_ref[...] = x_ref[...] + gf * _rms(y_ref[...], g_ref[...])


def post_ffn_call(y, x, gain, gf, n_lat):
    m = y.shape[0]
    row = pl.BlockSpec((ROW_TILE, D_MODEL), lambda i: (i, 0))
    return pl.pallas_call(
        functools.partial(_post_ffn_kernel, ctx_tile=n_lat // ROW_TILE),
        out_shape=jax.ShapeDtypeStruct((m, D_MODEL), F32),
        grid=(m // ROW_TILE,),
        in_specs=[row, row, pl.BlockSpec((1, D_MODEL), lambda i: (0, 0)), pl.BlockSpec((2, D_MODEL), lambda i: (0, 0))],
        out_specs=row,
        compiler_params=_cparams(("arbitrary",)),
        name="post_ffn",
    )(y, x, gain, gf)


def _pack_bf16_pair(lo, hi):
    lo_bits = lax.bitcast_convert_type(lo, jnp.uint32) >> 16
    hi_bits = lax.bitcast_convert_type(hi, jnp.uint32) & jnp.uint32(0xFFFF0000)
    return lo_bits | hi_bits


def _unpack_bf16_pair(w):
    lo = lax.bitcast_convert_type(w << 16, F32).astype(BF16)
    hi = lax.bitcast_convert_type(w & jnp.uint32(0xFFFF0000), F32).astype(BF16)
    return lo, hi


def _moe_up_kernel(idx_ref, h_hbm, wg_ref, wu_ref, o_ref, xbuf, sem, *, cap, n_ff_tiles):
    e = pl.program_id(0)
    n_e = pl.num_programs(0)
    half = D_MODEL // 2

    f = pl.program_id(1)
    n_f = pl.num_programs(1)
    slot = e % 2
    part = cap // n_ff_tiles

    def row_copy(expert, s, dst_slot):
        row = idx_ref[expert, s]
        return pltpu.make_async_copy(h_hbm.at[pl.ds(row, 1), :], xbuf.at[dst_slot, pl.ds(s, 1), :], sem.at[dst_slot])

    def wait_rows(dst_slot):
        pltpu.make_async_copy(h_hbm.at[pl.ds(0, cap), :], xbuf.at[dst_slot], sem.at[dst_slot]).wait()

    @pl.when(f == 0)
    def _():
        @pl.when(e == 0)
        def _():
            def body(s, carry):
                row_copy(0, s, 0).start()
                return carry
            lax.fori_loop(0, cap, body, 0)

        wait_rows(slot)

    nxt = jnp.minimum(e + 1, n_e - 1)
    for j in range(part):
        row_copy(nxt, f * part + j, 1 - slot).start()

    x_lo, x_hi = _unpack_bf16_pair(xbuf[slot])

    def proj(w_ref):
        return (jnp.dot(x_lo, w_ref[0, 0, 0:half, :].astype(BF16), preferred_element_type=F32)
                + jnp.dot(x_hi, w_ref[0, 0, half:, :].astype(BF16), preferred_element_type=F32))

    o_ref[0] = (_silu(proj(wg_ref)) * proj(wu_ref)).astype(BF16)

    @pl.when(jnp.logical_and(e == n_e - 1, f == n_f - 1))
    def _():
        wait_rows(1 - slot)


def moe_up_call(idx, h_packed, w_gate, w_up, l, tf=256):
    e, cap = idx.shape
    d = D_MODEL
    return pl.pallas_call(
        functools.partial(_moe_up_kernel, cap=cap, n_ff_tiles=EXPERT_FF // tf),
        out_shape=jax.ShapeDtypeStruct((e, cap, EXPERT_FF), BF16),
        grid_spec=pltpu.PrefetchScalarGridSpec(
            num_scalar_prefetch=1,
            grid=(e, EXPERT_FF // tf),
            in_specs=[pl.BlockSpec(memory_space=pl.ANY),
                      pl.BlockSpec((1, 1, d, tf), lambda i, f, idx_ref: (l, i, 0, f)),
                      pl.BlockSpec((1, 1, d, tf), lambda i, f, idx_ref: (l, i, 0, f))],
            out_specs=pl.BlockSpec((1, cap, tf), lambda i, f, idx_ref: (i, 0, f)),
            scratch_shapes=[pltpu.VMEM((2, cap, d // 2), jnp.uint32), pltpu.SemaphoreType.DMA((2,))]),
        compiler_params=_cparams(("arbitrary", "arbitrary")),
        name="moe_up",
    )(idx, h_packed, w_gate, w_up)


def _moe_scatter_kernel(idx_ref, y_ref, acc_in, acc_hbm, buf, sem_in, sem_out, *, rows):
    del acc_in
    e = pl.program_id(0)
    base = pl.program_id(1) * rows

    def gather(s, carry):
        row = idx_ref[e, base + s]
        pltpu.make_async_copy(acc_hbm.at[pl.ds(row, 1), :], buf.at[pl.ds(s, 1), :], sem_in).start()
        return carry

    def scatter(s, carry):
        row = idx_ref[e, base + s]
        pltpu.make_async_copy(buf.at[pl.ds(s, 1), :], acc_hbm.at[pl.ds(row, 1), :], sem_out).start()
        return carry

    lax.fori_loop(0, rows, gather, 0)
    pltpu.make_async_copy(acc_hbm.at[pl.ds(0, rows), :], buf, sem_in).wait()
    buf[...] = buf[...] + y_ref[0]
    lax.fori_loop(0, rows, scatter, 0)
    pltpu.make_async_copy(buf, acc_hbm.at[pl.ds(0, rows), :], sem_out).wait()


def moe_scatter_call(idx, y_e, m):
    e, cap, d = y_e.shape
    rows = cap // 2
    return pl.pallas_call(
        functools.partial(_moe_scatter_kernel, rows=rows),
        out_shape=jax.ShapeDtypeStruct((m, d), F32),
        grid_spec=pltpu.PrefetchScalarGridSpec(
            num_scalar_prefetch=1,
            grid=(e, 2),
            in_specs=[pl.BlockSpec((1, rows, d), lambda i, hf, idx_ref: (i, hf, 0)),
                      pl.BlockSpec(memory_space=pl.ANY)],
            out_specs=pl.BlockSpec(memory_space=pl.ANY),
            scratch_shapes=[pltpu.VMEM((rows, d), F32), pltpu.SemaphoreType.DMA(()), pltpu.SemaphoreType.DMA(())]),
        input_output_aliases={2: 0},
        compiler_params=_cparams(("arbitrary", "arbitrary")),
        name="moe_scatter",
    )(idx, y_e, jnp.zeros((m, d), F32))


def _moe_down_kernel(h_ref, wd_ref, g_ref, o_ref):
    y = jnp.dot(h_ref[0], wd_ref[0, 0].astype(BF16), preferred_element_type=F32)
    o_ref[0] = y * g_ref[0]


def moe_down_call(hid, w_down, l, gates, tn=512):
    e, c, f = hid.shape
    d = w_down.shape[3]
    return pl.pallas_call(
        _moe_down_kernel,
        out_shape=jax.ShapeDtypeStruct((e, c, d), F32),
        grid=(e, d // tn),
        in_specs=[pl.BlockSpec((1, c, f), lambda i, n: (i, 0, 0)),
                  pl.BlockSpec((1, 1, f, tn), lambda i, n: (l, i, 0, n)),
                  pl.BlockSpec((1, c, 1), lambda i, n: (i, 0, 0))],
        out_specs=pl.BlockSpec((1, c, tn), lambda i, n: (i, 0, n)),
        compiler_params=_cparams(("arbitrary", "arbitrary")),
        name="moe_down",
    )(hid, w_down, gates)


def _rope_tables(n_lat, n_ctx):
    rows = n_lat // GRID_W
    row = jnp.broadcast_to(jnp.arange(rows, dtype=F32)[:, None], (rows, GRID_W)).reshape(-1)
    col = jnp.broadcast_to(jnp.arange(GRID_W, dtype=F32)[None, :], (rows, GRID_W)).reshape(-1)
    n_freq = MLA_ROPE // 4
    inv_freq = ROPE_THETA ** (-jnp.arange(n_freq, dtype=F32) / n_freq)
    ang = jnp.concatenate([row[:, None] * inv_freq, col[:, None] * inv_freq], axis=-1)
    cos = jnp.cos(ang)
    sin = jnp.sin(ang)
    cos64 = jnp.concatenate([cos, cos], axis=-1)
    sin64 = jnp.concatenate([sin, sin], axis=-1)
    one_l = jnp.ones((n_lat, 64), F32)
    zero_l = jnp.zeros((n_lat, 64), F32)
    one_c = jnp.ones((n_ctx, 64), F32)
    zero_c = jnp.zeros((n_ctx, 64), F32)
    cq = jnp.concatenate([jnp.concatenate([cos64, one_l], 1), jnp.concatenate([one_c, one_c], 1)], 0)
    sq = jnp.concatenate([jnp.concatenate([sin64, zero_l], 1), jnp.concatenate([zero_c, zero_c], 1)], 0)
    ck = jnp.concatenate([jnp.concatenate([cos64, zero_l], 1), jnp.concatenate([zero_c, one_c], 1)], 0)
    sk = jnp.concatenate([jnp.concatenate([sin64, zero_l], 1), jnp.concatenate([zero_c, zero_c], 1)], 0)
    return cq, sq, ck, sk


def _rot_half_cols(w):
    half = MLA_ROPE // 2
    return jnp.concatenate([-w[..., half:], w[..., :half]], axis=-1)


def _relayout_w_in(w_in):
    q_a = w_in[:, 0:1024]
    kv_a = w_in[:, 1024:1536]
    k_pe = w_in[:, 1536:1600]
    g_qkv = w_in[:, 1600:4672]
    g_z = w_in[:, 4672:5696]
    g_ba = w_in[:, 5696:5728]
    r_x = w_in[:, 5728:6752]
    r_y = w_in[:, 6752:7776]
    zeros = functools.partial(jnp.zeros, dtype=w_in.dtype)
    d = w_in.shape[0]
    return jnp.concatenate([q_a, g_qkv, g_z, r_x, r_y, kv_a, k_pe, k_pe, _rot_half_cols(k_pe), zeros((d, 64)),
                            g_ba, zeros((d, LANE - 32)), zeros((d, LANE))], axis=1)


def _relayout_w_qb(w_qb):
    w = w_qb.reshape(MLA_Q_RANK, MLA_HEADS, MLA_NOPE + MLA_ROPE)
    nope = w[:, :, :MLA_NOPE]
    pe = w[:, :, MLA_NOPE:]
    zeros = jnp.zeros_like(pe)
    out = jnp.concatenate([nope, pe, pe, _rot_half_cols(pe), zeros], axis=-1)
    return jnp.transpose(out, (1, 0, 2))


BIG_WEIGHTS = ('mod_w', 'mla_w_kvb', 'w_out', 'w_gate', 'w_up', 'w_down')


def _layer(x_all, c_rows, p, big, l, n_lat, n_ctx, update_ctx):
    t = n_lat + n_ctx
    mod = mod_call(c_rows, big['mod_w'], p['mod_b'][None, :], l)[0:2]
    sh_m, sc_m, g_m, sh_f, sc_f, g_f = [mod[:, i * D_MODEL:(i + 1) * D_MODEL] for i in range(6)]

    h = prenorm_call(x_all, p['norm_mix_pre'][None, :], sc_m, sh_m, n_lat)
    z = matmul_call(h, _relayout_w_in(p['w_in']), name="w_in")

    cq, sq, ck, sk = _rope_tables(n_lat, n_ctx)
    q = qproj_call(z, p['mla_q_norm'][None, :], _relayout_w_qb(p['mla_w_qb']), cq, sq)
    tk = next(c for c in (768, 256) if t % c == 0)
    k, v = kvproj_call(z, p['mla_kv_norm'][None, :], big['mla_w_kvb'], l, ck, sk, tk)
    o_mla = attn_call(q, k, v, n_lat, 0, t, 0, tq=512, tk=tk)
    if update_ctx:
        o_mla_c = attn_call(q, k, v, n_ctx, n_lat, n_ctx, n_lat, tq=n_ctx, tk=n_ctx)
        o_mla = jnp.concatenate([o_mla, o_mla_c], axis=0)

    pad = jnp.zeros((2 * GDN_HEADS,), F32)
    alog_row = jnp.concatenate([pad, p['gdn_a_log'].reshape(-1), jnp.zeros((LANE - 4 * GDN_HEADS,), F32)])[None, :]
    dtb_row = jnp.concatenate([pad, p['gdn_dt_bias'].reshape(-1), jnp.zeros((LANE - 4 * GDN_HEADS,), F32)])[None, :]
    qkv, bg = gdn_prep_call(z, p['gdn_conv_w'], alog_row, dtb_row, n_lat)
    gt = jnp.swapaxes(bg[:, 2 * GDN_HEADS:4 * GDN_HEADS].reshape(t // GDN_CHUNK, GDN_CHUNK, 2 * GDN_HEADS), 1, 2)
    o_f, o_r = gdn_call(qkv, bg, gt, n_lat)

    h_f, h_r = rg_call(z, p['rg_conv_w'], p['rg_conv_b'][None, :], p['rg_w_a'], p['rg_b_a'], p['rg_w_x'],
                       p['rg_b_x'], p['rg_lambda'], n_lat)

    m = t if update_ctx else n_lat
    o_gdn, o_rg = assemble_call(o_f, o_r, z, p['gdn_o_norm'][None, :], h_f, h_r, m)
    y = matmul3_call(o_mla, o_gdn, o_rg, big['w_out'], l, m)
    x1, h2, aff_t = post_mix_call(y, x_all, p['norm_mix_post'][None, :], g_m, p['norm_ffn_pre'][None, :],
                                  sc_f, sh_f, p['w_router'].T, n_lat)

    cap = CAPACITY_FACTOR * n_lat // N_EXPERTS
    gates, idx = lax.top_k(aff_t[:, :n_lat], cap)
    if update_ctx:
        cap_c = CAPACITY_FACTOR * n_ctx // N_EXPERTS
        gates_c, idx_c = lax.top_k(aff_t[:, n_lat:], cap_c)
        gates = jnp.concatenate([gates, gates_c], axis=1)
        idx = jnp.concatenate([idx, idx_c + n_lat], axis=1)
    hid = moe_up_call(idx, h2, big['w_gate'], big['w_up'], l)
    y_e = moe_down_call(hid, big['w_down'], l, gates[..., None])
    y_moe = moe_scatter_call(idx, y_e, m)
    return post_ffn_call(y_moe, x1, p['norm_ffn_post'][None, :], g_f, n_lat)


def kernel(x, c, ctx, c_ctx, mod_w, mod_b, norm_mix_pre, norm_mix_post, norm_ffn_pre, norm_ffn_post,
           w_in, mla_q_norm, mla_kv_norm, mla_w_qb, mla_w_kvb, gdn_conv_w, gdn_a_log, gdn_dt_bias,
           gdn_o_norm, rg_conv_w, rg_conv_b, rg_w_a, rg_b_a, rg_w_x, rg_b_x, rg_lambda, w_out,
           w_router, w_gate, w_up, w_down):
    assert x.shape[0] == 1 and ctx.shape[0] == 1
    n_lat, n_ctx = x.shape[1], ctx.shape[1]
    assert n_ctx == ROW_TILE and n_lat % 512 == 0
    depth = mod_w.shape[0]
    stacked = dict(mod_w=mod_w, mod_b=mod_b, norm_mix_pre=norm_mix_pre, norm_mix_post=norm_mix_post,
                   norm_ffn_pre=norm_ffn_pre, norm_ffn_post=norm_ffn_post, w_in=w_in, mla_q_norm=mla_q_norm,
                   mla_kv_norm=mla_kv_norm, mla_w_qb=mla_w_qb, mla_w_kvb=mla_w_kvb, gdn_conv_w=gdn_conv_w,
                   gdn_a_log=gdn_a_log, gdn_dt_bias=gdn_dt_bias, gdn_o_norm=gdn_o_norm, rg_conv_w=rg_conv_w,
                   rg_conv_b=rg_conv_b, rg_w_a=rg_w_a, rg_b_a=rg_b_a, rg_w_x=rg_w_x, rg_b_x=rg_b_x,
                   rg_lambda=rg_lambda, w_out=w_out, w_router=w_router, w_gate=w_gate, w_up=w_up, w_down=w_down)
    c_rows = jnp.concatenate([c, c_ctx[None, :], jnp.zeros((6, D_MODEL), F32)], axis=0)
    x_all = jnp.concatenate([x[0], ctx[0]], axis=0)
    big = {k: stacked[k] for k in BIG_WEIGHTS}
    for l in range(depth):
        p = {k: v[l] for k, v in stacked.items() if k not in BIG_WEIGHTS}
        x_all = _layer(x_all, c_rows, p, big, l, n_lat, n_ctx, l < depth - 1)
    return x_all[None]
```

```python
import functools
import math

import jax
import jax.numpy as jnp
from jax import lax
from jax.experimental import pallas as pl
from jax.experimental.pallas import tpu as pltpu

F32 = jnp.float32
BF16 = jnp.bfloat16

D_MODEL = 4096
EPS = 1e-6
GRID_W = 64
ROPE_THETA = 10000.0

MLA_HEADS = 16
MLA_NOPE = 128
MLA_ROPE = 64
MLA_V = 128
MLA_Q_RANK = 1024
MLA_KV_RANK = 512
VT_ROWS = MLA_V + 16

GDN_HEADS = 8
GDN_DK = 128
GDN_DV = 128
GDN_CHUNK = 64
GDN_W = GDN_HEADS * GDN_DK

RG_WIDTH = 1024
RG_BLOCKS = 8
RG_BS = RG_WIDTH // RG_BLOCKS
RG_C = 8.0

N_EXPERTS = 16
EXPERT_FF = 1024
CAPACITY_FACTOR = 2

LANE = 128
SUBLANES = 8
ROW_TILE = 256
VMEM_LIMIT = 56 * 1024 * 1024

Z_QA, Z_GQ, Z_GK, Z_GV, Z_GZ, Z_RX, Z_RY = 0, 1024, 2048, 3072, 4096, 5120, 6144
Z_KVA = 7168
Z_PEA, Z_PEB, Z_BA = 7680, 7808, 7936
Z_COLS = 8192


def _cparams(sem, vmem=VMEM_LIMIT):
    return pltpu.CompilerParams(dimension_semantics=sem, vmem_limit_bytes=vmem)


def _sigmoid(x):
    return 1.0 / (1.0 + jnp.exp(-x))


def _silu(x):
    return x * _sigmoid(x)


def _softplus(x):
    return jnp.maximum(x, 0.0) + jnp.log(1.0 + jnp.exp(-jnp.abs(x)))


def _one_minus_exp(y):
    poly = 1.0 / math.factorial(8)
    for j in range(6, -1, -1):
        poly = poly * y + 1.0 / math.factorial(j + 1)
    return jnp.where(y > -0.125, -(y * poly), 1.0 - jnp.exp(y))


def _gelu_tanh(x):
    return 0.5 * x * (1.0 + jnp.tanh(math.sqrt(2.0 / math.pi) * (x + 0.044715 * (x * x * x))))


def _rms(x, gain):
    return x * lax.rsqrt(jnp.mean(x * x, axis=-1, keepdims=True) + EPS) * gain


def _mod_kernel(c_ref, w_ref, b_ref, o_ref):
    s = _silu(c_ref[...])
    o_ref[...] = jnp.dot(s.astype(BF16), w_ref[0].astype(BF16), preferred_element_type=F32) + b_ref[...]


def mod_call(cc, w, b, l):
    tn = 1024
    n = w.shape[2]
    return pl.pallas_call(
        _mod_kernel,
        out_shape=jax.ShapeDtypeStruct((8, n), F32),
        grid=(n // tn,),
        in_specs=[pl.BlockSpec((8, D_MODEL), lambda j: (0, 0)),
                  pl.BlockSpec((1, D_MODEL, tn), lambda j: (l, 0, j)),
                  pl.BlockSpec((1, tn), lambda j: (0, j))],
        out_specs=pl.BlockSpec((8, tn), lambda j: (0, j)),
        compiler_params=_cparams(("arbitrary",)),
        name="mod",
    )(cc, w, b)


def _prenorm_kernel(x_ref, g_ref, sc_ref, sh_ref, o_ref, *, ctx_tile):
    is_ctx = pl.program_id(0) >= ctx_tile
    sc = jnp.where(is_ctx, sc_ref[1:2, :], sc_ref[0:1, :])
    sh = jnp.where(is_ctx, sh_ref[1:2, :], sh_ref[0:1, :])
    y = _rms(x_ref[...], g_ref[...])
    o_ref[...] = (y * (1.0 + sc) + sh).astype(BF16)


def prenorm_call(x, gain, sc, sh, n_lat):
    t = x.shape[0]
    row = pl.BlockSpec((ROW_TILE, D_MODEL), lambda i: (i, 0))
    return pl.pallas_call(
        functools.partial(_prenorm_kernel, ctx_tile=n_lat // ROW_TILE),
        out_shape=jax.ShapeDtypeStruct((t, D_MODEL), BF16),
        grid=(t // ROW_TILE,),
        in_specs=[row, pl.BlockSpec((1, D_MODEL), lambda i: (0, 0)),
                  pl.BlockSpec((2, D_MODEL), lambda i: (0, 0)), pl.BlockSpec((2, D_MODEL), lambda i: (0, 0))],
        out_specs=row,
        compiler_params=_cparams(("arbitrary",)),
        name="prenorm",
    )(x, gain, sc, sh)


def _matmul_kernel(a_ref, w_ref, o_ref):
    o_ref[...] = jnp.dot(a_ref[...], w_ref[...].astype(BF16), preferred_element_type=F32)


def _pick_tm(m):
    for tm in (1056, 1024, 768, 512, 256):
        if m % tm == 0:
            return tm
    raise ValueError(f"unsupported row count {m}")


def matmul_call(a, w, tn=512, name="matmul"):
    m, k = a.shape
    n = w.shape[1]
    tm = _pick_tm(m)
    return pl.pallas_call(
        _matmul_kernel,
        out_shape=jax.ShapeDtypeStruct((m, n), F32),
        grid=(m // tm, n // tn),
        in_specs=[pl.BlockSpec((tm, k), lambda i, j: (i, 0)), pl.BlockSpec((k, tn), lambda i, j: (0, j))],
        out_specs=pl.BlockSpec((tm, tn), lambda i, j: (i, j)),
        compiler_params=_cparams(("arbitrary", "arbitrary")),
        name=name,
    )(a, w)


def _matmul3_kernel(a1_ref, a2_ref, a3_ref, w_ref, o_ref):
    k1 = a1_ref.shape[1]
    k2 = a2_ref.shape[1]
    acc = jnp.dot(a1_ref[...], w_ref[0, 0:k1, :].astype(BF16), preferred_element_type=F32)
    acc += jnp.dot(a2_ref[...], w_ref[0, k1:k1 + k2, :].astype(BF16), preferred_element_type=F32)
    acc += jnp.dot(a3_ref[...], w_ref[0, k1 + k2:, :].astype(BF16), preferred_element_type=F32)
    o_ref[...] = acc


def matmul3_call(a1, a2, a3, w, l, m, tn=512):
    k = w.shape[1]
    n = w.shape[2]
    tm = _pick_tm(m)
    return pl.pallas_call(
        _matmul3_kernel,
        out_shape=jax.ShapeDtypeStruct((m, n), F32),
        grid=(m // tm, n // tn),
        in_specs=[pl.BlockSpec((tm, a1.shape[1]), lambda i, j: (i, 0)),
                  pl.BlockSpec((tm, a2.shape[1]), lambda i, j: (i, 0)),
                  pl.BlockSpec((tm, a3.shape[1]), lambda i, j: (i, 0)),
                  pl.BlockSpec((1, k, tn), lambda i, j: (l, 0, j))],
        out_specs=pl.BlockSpec((tm, tn), lambda i, j: (i, j)),
        compiler_params=_cparams(("arbitrary", "arbitrary")),
        name="w_out",
    )(a1, a2, a3, w)


def _qproj_kernel(z_ref, g_ref, w_ref, c_ref, s_ref, o_ref, zn_ref, *, scale):
    @pl.when(pl.program_id(1) == 0)
    def _():
        zn_ref[...] = _rms(z_ref[...], g_ref[...]).astype(BF16)

    r = jnp.dot(zn_ref[...], w_ref[0].astype(BF16), preferred_element_type=F32)
    q0 = r[:, 0:LANE]
    q1 = r[:, LANE:2 * LANE] * c_ref[...] + r[:, 2 * LANE:3 * LANE] * s_ref[...]
    o_ref[0, :, 0:LANE] = (q0 * scale).astype(BF16)
    o_ref[0, :, LANE:2 * LANE] = (q1 * scale).astype(BF16)


def qproj_call(z, q_norm, wq, cq, sq):
    t = z.shape[0]
    tm = _pick_tm(t)
    scale = (MLA_NOPE + MLA_ROPE) ** -0.5 * math.log2(math.e)
    return pl.pallas_call(
        functools.partial(_qproj_kernel, scale=scale),
        out_shape=jax.ShapeDtypeStruct((MLA_HEADS, t, 2 * LANE), BF16),
        grid=(t // tm, MLA_HEADS),
        in_specs=[pl.BlockSpec((tm, MLA_Q_RANK), lambda i, h: (i, Z_QA // MLA_Q_RANK)),
                  pl.BlockSpec((1, MLA_Q_RANK), lambda i, h: (0, 0)),
                  pl.BlockSpec((1, MLA_Q_RANK, 3 * LANE), lambda i, h: (h, 0, 0)),
                  pl.BlockSpec((tm, LANE), lambda i, h: (i, 0)),
                  pl.BlockSpec((tm, LANE), lambda i, h: (i, 0))],
        out_specs=pl.BlockSpec((1, tm, 2 * LANE), lambda i, h: (h, i, 0)),
        scratch_shapes=[pltpu.VMEM((tm, MLA_Q_RANK), BF16)],
        compiler_params=_cparams(("arbitrary", "arbitrary")),
        name="q_proj",
    )(z, q_norm, wq, cq, sq)


KV_HEAD_GROUP = 4


def _kvproj_kernel(z_ref, g_ref, w_ref, pa_ref, pb_ref, c_ref, s_ref, k_ref, v_ref, zn_ref):
    @pl.when(pl.program_id(1) == 0)
    def _():
        zn_ref[...] = _rms(z_ref[...], g_ref[...]).astype(BF16)

    r = jnp.dot(zn_ref[...], w_ref[0].astype(BF16), preferred_element_type=F32)
    k1 = (pa_ref[...] * c_ref[...] + pb_ref[...] * s_ref[...]).astype(BF16)
    for g in range(KV_HEAD_GROUP):
        c0 = g * 2 * LANE
        k_ref[g, :, 0:LANE] = r[:, c0:c0 + LANE].astype(BF16)
        k_ref[g, :, LANE:2 * LANE] = k1
        v_ref[g, 0, 0:MLA_V, :] = r[:, c0 + LANE:c0 + 2 * LANE].T.astype(BF16)
        v_ref[g, 0, MLA_V:VT_ROWS, :] = jnp.ones((VT_ROWS - MLA_V, r.shape[0]), BF16)


def kvproj_call(z, kv_norm, wkv, l, ck, sk, tm):
    t = z.shape[0]
    return pl.pallas_call(
        _kvproj_kernel,
        out_shape=(jax.ShapeDtypeStruct((MLA_HEADS, t, 2 * LANE), BF16),
                   jax.ShapeDtypeStruct((MLA_HEADS, t // tm, VT_ROWS, tm), BF16)),
        grid=(t // tm, MLA_HEADS // KV_HEAD_GROUP),
        in_specs=[pl.BlockSpec((tm, MLA_KV_RANK), lambda i, h: (i, Z_KVA // MLA_KV_RANK)),
                  pl.BlockSpec((1, MLA_KV_RANK), lambda i, h: (0, 0)),
                  pl.BlockSpec((1, MLA_KV_RANK, KV_HEAD_GROUP * 2 * LANE), lambda i, h: (l, 0, h)),
                  pl.BlockSpec((tm, LANE), lambda i, h: (i, Z_PEA // LANE)),
                  pl.BlockSpec((tm, LANE), lambda i, h: (i, Z_PEB // LANE)),
                  pl.BlockSpec((tm, LANE), lambda i, h: (i, 0)),
                  pl.BlockSpec((tm, LANE), lambda i, h: (i, 0))],
        out_specs=(pl.BlockSpec((KV_HEAD_GROUP, tm, 2 * LANE), lambda i, h: (h, i, 0)),
                   pl.BlockSpec((KV_HEAD_GROUP, 1, VT_ROWS, tm), lambda i, h: (h, i, 0, 0))),
        scratch_shapes=[pltpu.VMEM((tm, MLA_KV_RANK), BF16)],
        compiler_params=_cparams(("arbitrary", "arbitrary")),
        name="kv_proj",
    )(z, kv_norm, wkv, z, z, ck, sk)


ATTN_LOOKAHEAD = 2


def _attn_kernel(q_ref, k_ref, vt_ref, o_ref, m_ref, acc_ref, *bufs, tk, n_chunks):
    q = q_ref[0]
    nt = (((1,), (1,)), ((), ()))

    def scores(c):
        return lax.dot_general(k_ref[0, c * tk:(c + 1) * tk, :], q, nt, preferred_element_type=F32)

    def consume(c, s_ref):
        m_old = m_ref[...]
        m_new = jnp.maximum(m_old, jnp.max(s_ref[...], axis=0, keepdims=True))
        alpha = jnp.exp2(m_old - m_new)
        p = jnp.exp2(s_ref[...] - m_new)
        pv = jnp.dot(vt_ref[0, c], p.astype(BF16), preferred_element_type=F32)
        acc_ref[...] = alpha * acc_ref[...] + pv
        m_ref[...] = m_new

    m_ref[...] = jnp.full(m_ref.shape, -jnp.inf, F32)
    acc_ref[...] = jnp.zeros(acc_ref.shape, F32)
    for c in range(min(ATTN_LOOKAHEAD, n_chunks)):
        bufs[c][...] = scores(c)
    for c in range(n_chunks):
        if c + ATTN_LOOKAHEAD < n_chunks:
            bufs[(c + ATTN_LOOKAHEAD) % len(bufs)][...] = scores(c + ATTN_LOOKAHEAD)
        consume(c, bufs[c % len(bufs)])
    o_ref[...] = (acc_ref[0:MLA_V, :] / acc_ref[MLA_V:MLA_V + 1, :]).T.astype(BF16)


def attn_call(q, k, vt, n_q, q_row0, n_k, k_row0, tq, tk):
    ck = vt.shape[3]
    assert n_q % tq == 0 and q_row0 % tq == 0 and n_k % tk == 0 and k_row0 % n_k == 0
    n_chunks = n_k // tk
    if n_chunks > 1:
        assert tk == ck and k_row0 == 0
        vt_spec = pl.BlockSpec((1, n_chunks, VT_ROWS, ck), lambda h, j: (h, 0, 0, 0))
    else:
        assert (k_row0 % ck) % tk == 0
        vt_spec = pl.BlockSpec((1, 1, VT_ROWS, tk), lambda h, j: (h, k_row0 // ck, 0, (k_row0 % ck) // tk))
    qb, kb = q_row0 // tq, k_row0 // n_k
    return pl.pallas_call(
        functools.partial(_attn_kernel, tk=tk, n_chunks=n_chunks),
        out_shape=jax.ShapeDtypeStruct((n_q, MLA_HEADS * MLA_V), BF16),
        grid=(MLA_HEADS, n_q // tq),
        in_specs=[pl.BlockSpec((1, tq, 2 * LANE), lambda h, j: (h, j + qb, 0)),
                  pl.BlockSpec((1, n_k, 2 * LANE), lambda h, j: (h, kb, 0)),
                  vt_spec],
        out_specs=pl.BlockSpec((tq, MLA_V), lambda h, j: (j, h)),
        scratch_shapes=[pltpu.VMEM((1, tq), F32), pltpu.VMEM((VT_ROWS, tq), F32)]
        + [pltpu.VMEM((tk, tq), F32)] * (ATTN_LOOKAHEAD + 1),
        compiler_params=_cparams(("arbitrary", "arbitrary")),
        name="mla_attn",
    )(q, k, vt)


def _conv4(x_ref, p_ref, n_ref, w, ext_ref, has_prev, has_next):
    t = x_ref.shape[0]
    ext_ref[0:8, :] = jnp.where(has_prev, p_ref[...], 0.0)
    ext_ref[8:8 + t, :] = x_ref[...]
    ext_ref[8 + t:16 + t, :] = jnp.where(has_next, n_ref[...], 0.0)
    y = ext_ref[6:6 + t, :] * w[0:1, :]
    y += ext_ref[7:7 + t, :] * w[1:2, :]
    y += ext_ref[8:8 + t, :] * w[2:3, :]
    y += ext_ref[9:9 + t, :] * w[3:4, :]
    return y


def _halo_flags(i, n_lat_tiles):
    has_prev = jnp.logical_and(i > 0, i < n_lat_tiles)
    has_next = i < n_lat_tiles - 1
    return has_prev, has_next


def _halo_specs(width, col_of, n_tiles):
    r8 = ROW_TILE // 8
    last8 = n_tiles * r8 - 1
    main = pl.BlockSpec((ROW_TILE, width), lambda i, *a: (i, col_of(*a)))
    prev = pl.BlockSpec((8, width), lambda i, *a: (jnp.maximum(i * r8 - 1, 0), col_of(*a)))
    nxt = pl.BlockSpec((8, width), lambda i, *a: (jnp.minimum((i + 1) * r8, last8), col_of(*a)))
    return main, prev, nxt


def _gdn_prep_kernel(x_ref, p_ref, n_ref, w_ref, ba_ref, alog_ref, dtb_ref, o_ref, bg_ref, ext_ref, *, n_lat_tiles):
    i = pl.program_id(0)
    j = pl.program_id(1)
    has_prev, has_next = _halo_flags(i, n_lat_tiles)
    y = _silu(_conv4(x_ref, p_ref, n_ref, w_ref[...], ext_ref, has_prev, has_next))
    qscale = jnp.where(j == 0, GDN_DK ** -0.5, 1.0)
    is_v = j == 2
    for h in range(GDN_HEADS):
        seg = y[:, h * LANE:(h + 1) * LANE]
        nrm = seg * lax.rsqrt(jnp.sum(seg * seg, axis=-1, keepdims=True) + EPS) * qscale
        o_ref[0, :, h * LANE:(h + 1) * LANE] = jnp.where(is_v, seg, nrm)

    @pl.when(j == 0)
    def _():
        ba = ba_ref[...]
        lane = lax.broadcasted_iota(jnp.int32, ba.shape, 1)
        beta = _sigmoid(ba)
        g = -jnp.exp(alog_ref[...]) * _softplus(ba + dtb_ref[...])
        bg_ref[...] = jnp.where(lane < 2 * GDN_HEADS, beta, g)


def gdn_prep_call(z, conv_w, alog_row, dtb_row, n_lat):
    t = z.shape[0]
    n_tiles = t // ROW_TILE
    main, prev, nxt = _halo_specs(GDN_W, lambda j: Z_GQ // GDN_W + j, n_tiles)
    row128 = pl.BlockSpec((1, LANE), lambda i, j: (0, 0))
    return pl.pallas_call(
        functools.partial(_gdn_prep_kernel, n_lat_tiles=n_lat // ROW_TILE),
        out_shape=(jax.ShapeDtypeStruct((3, t, GDN_W), F32), jax.ShapeDtypeStruct((t, LANE), F32)),
        grid=(n_tiles, 3),
        in_specs=[main, prev, nxt,
                  pl.BlockSpec((4, GDN_W), lambda i, j: (0, j)),
                  pl.BlockSpec((ROW_TILE, LANE), lambda i, j: (i, Z_BA // LANE)),
                  row128, row128],
        out_specs=(pl.BlockSpec((1, ROW_TILE, GDN_W), lambda i, j: (j, i, 0)),
                   pl.BlockSpec((ROW_TILE, LANE), lambda i, j: (i, 0))),
        scratch_shapes=[pltpu.VMEM((ROW_TILE + 16, GDN_W), F32)],
        compiler_params=_cparams(("arbitrary", "arbitrary")),
        name="gdn_prep",
    )(z, z, z, conv_w, z, alog_row, dtb_row)


def _split3(x):
    hi = x.astype(BF16)
    r1 = x - hi.astype(F32)
    mid = r1.astype(BF16)
    lo = (r1 - mid.astype(F32)).astype(BF16)
    return hi, mid, lo


def _bdot(a, b):
    return jnp.dot(a.astype(BF16), b.astype(BF16), preferred_element_type=F32)


def _gdn_gates(q_ref, k_ref, v_ref, bg_ref, gt_ref, o_ref, d):
    c = GDN_CHUNK
    row = lax.broadcasted_iota(jnp.int32, (c, c), 0)
    col = lax.broadcasted_iota(jnp.int32, (c, c), 1)
    if d == 0:
        incl = row >= col
        strict = row > col
    else:
        incl = row <= col
        strict = row < col
    incl_t = (row <= col) if d == 0 else (row >= col)
    tri = jnp.where(incl, 1.0, 0.0).astype(BF16)
    tri_t = jnp.where(incl_t, 1.0, 0.0).astype(BF16)
    bg = bg_ref[...]
    g_hi, g_mid, g_lo = _split3(bg)
    gc_all = (jnp.dot(tri, g_hi, preferred_element_type=F32) + jnp.dot(tri, g_mid, preferred_element_type=F32)
              + jnp.dot(tri, g_lo, preferred_element_type=F32))
    t_hi, t_mid, t_lo = _split3(gt_ref[0])
    gr_all = (jnp.dot(t_hi, tri_t, preferred_element_type=F32) + jnp.dot(t_mid, tri_t, preferred_element_type=F32)
              + jnp.dot(t_lo, tri_t, preferred_element_type=F32))
    last = c - 1 if d == 0 else 0
    heads = []
    for h in range(GDN_HEADS):
        vh = d * GDN_HEADS + h
        sl = slice(h * LANE, (h + 1) * LANE)
        gcol = gc_all[:, 2 * GDN_HEADS + vh:2 * GDN_HEADS + vh + 1]
        heads.append(dict(
            vh=vh, sl=sl, o_ref=o_ref, incl=incl, strict=strict,
            qh=q_ref[0, :, sl], kh=k_ref[0, :, sl], vv=v_ref[0, :, sl],
            beta=bg[:, vh:vh + 1],
            gcol=gcol,
            grow=gr_all[vh:vh + 1, :],
            g_last=gcol[last:last + 1, :]))
    return heads


def _gdn_kernel(qf_ref, kf_ref, vf_ref, bgf_ref, gtf_ref, qr_ref, kr_ref, vr_ref, bgr_ref, gtr_ref,
                of_ref, or_ref, s_ref):
    @pl.when(pl.program_id(0) == 0)
    def _():
        s_ref[...] = jnp.zeros(s_ref.shape, F32)

    c = GDN_CHUNK
    hs = (_gdn_gates(qf_ref, kf_ref, vf_ref, bgf_ref, gtf_ref, of_ref, 0)
          + _gdn_gates(qr_ref, kr_ref, vr_ref, bgr_ref, gtr_ref, or_ref, 1))
    row = lax.broadcasted_iota(jnp.int32, (c, c), 0)
    col = lax.broadcasted_iota(jnp.int32, (c, c), 1)
    eye = jnp.where(row == col, 1.0, 0.0)
    nt = (((1,), (1,)), ((), ()))
    for t in hs:
        kb = t['kh'].astype(BF16)
        t['qk_kk'] = lax.dot_general(jnp.concatenate([t['qh'].astype(BF16), kb], axis=0), kb, nt,
                                     preferred_element_type=F32)
    for t in hs:
        decay = jnp.exp(jnp.where(t['incl'], t['gcol'] - t['grow'], -jnp.inf))
        t['qk'] = t['qk_kk'][0:c] * decay
        a = jnp.where(t['strict'], t['qk_kk'][c:2 * c] * t['beta'] * decay, 0.0)
        t['pw'] = a
        t['inv'] = eye - a
    for _ in range(5):
        for t in hs:
            t['pw'] = _bdot(t['pw'], t['pw'])
        for t in hs:
            t['inv'] = t['inv'] + _bdot(t['inv'], t['pw'])
    for t in hs:
        eg = jnp.exp(t['gcol'])
        rhs = jnp.concatenate([t['vv'] * t['beta'], t['kh'] * (t['beta'] * eg)], axis=1)
        t['uw'] = _bdot(t['inv'], rhs)
        t['qd'] = t['qh'] * eg
    for t in hs:
        t['s_old'] = s_ref[t['vh']]
        t['ws_qs'] = _bdot(jnp.concatenate([t['uw'][:, LANE:2 * LANE], t['qd']], axis=0), t['s_old'])
    for t in hs:
        t['v_new'] = t['uw'][:, 0:LANE] - t['ws_qs'][0:c]
        t['o_ref'][:, t['sl']] = t['ws_qs'][c:2 * c] + _bdot(t['qk'], t['v_new'])
    for t in hs:
        kt = t['kh'] * jnp.exp(t['g_last'] - t['gcol'])
        s_ref[t['vh']] = t['s_old'] * jnp.exp(t['g_last']) + _bdot(kt.T, t['v_new'])


def gdn_call(qkv, bg, gt, n_lat):
    t = qkv.shape[1]
    c = GDN_CHUNK
    n = t // c
    n_ctx = (t - n_lat) // c

    def fwd(s):
        return (s + n - n_ctx) % n

    def rev(s):
        return n - 1 - s

    def specs(order):
        return [pl.BlockSpec((1, c, GDN_W), lambda s: (0, order(s), 0)),
                pl.BlockSpec((1, c, GDN_W), lambda s: (1, order(s), 0)),
                pl.BlockSpec((1, c, GDN_W), lambda s: (2, order(s), 0)),
                pl.BlockSpec((c, LANE), lambda s: (order(s), 0)),
                pl.BlockSpec((1, 2 * GDN_HEADS, c), lambda s: (order(s), 0, 0))]

    return pl.pallas_call(
        _gdn_kernel,
        out_shape=(jax.ShapeDtypeStruct((t, GDN_W), F32), jax.ShapeDtypeStruct((t, GDN_W), F32)),
        grid=(n,),
        in_specs=specs(fwd) + specs(rev),
        out_specs=(pl.BlockSpec((c, GDN_W), lambda s: (fwd(s), 0)), pl.BlockSpec((c, GDN_W), lambda s: (rev(s), 0))),
        scratch_shapes=[pltpu.VMEM((2 * GDN_HEADS, GDN_DK, GDN_DV), F32)],
        compiler_params=_cparams(("arbitrary",)),
        name="gdn_scan",
    )(qkv, qkv, qkv, bg, gt, qkv, qkv, qkv, bg, gt)


def _rg_direction(x_ref, p_ref, n_ref, cw_ref, cb_ref, wa_ref, ba_ref, wx_ref, bx_ref, lam_ref, o_ref,
                  h_ref, ext_ref, tile, n_lat_tiles, d):
    t = ROW_TILE
    has_prev, has_next = _halo_flags(tile, n_lat_tiles)
    x = _conv4(x_ref, p_ref, n_ref, cw_ref[...], ext_ref, has_prev, has_next) + cb_ref[...]
    xb = x.astype(BF16)
    r_parts, i_parts = [], []
    for n in range(RG_BLOCKS):
        seg = xb[:, n * RG_BS:(n + 1) * RG_BS]
        r_parts.append(jnp.dot(seg, wa_ref[0, n].astype(BF16), preferred_element_type=F32))
        i_parts.append(jnp.dot(seg, wx_ref[0, n].astype(BF16), preferred_element_type=F32))
    r = _sigmoid(jnp.concatenate(r_parts, axis=1) + ba_ref[0])
    ig = _sigmoid(jnp.concatenate(i_parts, axis=1) + bx_ref[0])
    log_a = -RG_C * r * _softplus(-lam_ref[0])
    a = jnp.exp(log_a)
    u = jnp.sqrt(_one_minus_exp(2.0 * log_a)) * (ig * x)
    sub = lax.broadcasted_iota(jnp.int32, (t, 1), 0) & (SUBLANES - 1)
    sh = 1
    while sh < SUBLANES:
        if d == 0:
            a_s = pltpu.roll(a, sh, 0)
            u_s = pltpu.roll(u, sh, 0)
            ok = sub >= sh
        else:
            a_s = pltpu.roll(a, t - sh, 0)
            u_s = pltpu.roll(u, t - sh, 0)
            ok = sub < SUBLANES - sh
        u = jnp.where(ok, a * u_s + u, u)
        a = jnp.where(ok, a * a_s, a)
        sh *= 2
    carry = h_ref[d:d + 1, :]
    n_groups = t // SUBLANES
    for g in (range(n_groups) if d == 0 else range(n_groups - 1, -1, -1)):
        rows = slice(g * SUBLANES, (g + 1) * SUBLANES)
        h_g = u[rows, :] + a[rows, :] * carry
        o_ref[rows, :] = h_g
        carry = h_g[SUBLANES - 1:SUBLANES, :] if d == 0 else h_g[0:1, :]
    h_ref[d:d + 1, :] = carry


def _rg_kernel(xf_ref, pf_ref, nf_ref, xr_ref, pr_ref, nr_ref, cw_ref, cb_ref,
               waf_ref, baf_ref, wxf_ref, bxf_ref, lamf_ref, war_ref, bar_ref, wxr_ref, bxr_ref, lamr_ref,
               of_ref, or_ref, h_ref, ext_ref, *, n_tiles, n_lat_tiles):
    s = pl.program_id(0)

    @pl.when(s == 0)
    def _():
        h_ref[...] = jnp.zeros(h_ref.shape, F32)

    tile_f = (s + n_lat_tiles) % n_tiles
    tile_r = n_tiles - 1 - s
    _rg_direction(xf_ref, pf_ref, nf_ref, cw_ref, cb_ref, waf_ref, baf_ref, wxf_ref, bxf_ref, lamf_ref, of_ref,
                  h_ref, ext_ref, tile_f, n_lat_tiles, 0)
    _rg_direction(xr_ref, pr_ref, nr_ref, cw_ref, cb_ref, war_ref, bar_ref, wxr_ref, bxr_ref, lamr_ref, or_ref,
                  h_ref, ext_ref, tile_r, n_lat_tiles, 1)


def rg_call(z, conv_w, conv_b, w_a, b_a, w_x, b_x, lam, n_lat):
    t = z.shape[0]
    n_tiles = t // ROW_TILE
    n_lat_tiles = n_lat // ROW_TILE
    r8 = ROW_TILE // 8
    last8 = n_tiles * r8 - 1
    colb = Z_RX // RG_WIDTH

    def fwd(s):
        return (s + n_lat_tiles) % n_tiles

    def rev(s):
        return n_tiles - 1 - s

    def tile_specs(order):
        return [pl.BlockSpec((ROW_TILE, RG_WIDTH), lambda s: (order(s), colb)),
                pl.BlockSpec((8, RG_WIDTH), lambda s: (jnp.maximum(order(s) * r8 - 1, 0), colb)),
                pl.BlockSpec((8, RG_WIDTH), lambda s: (jnp.minimum((order(s) + 1) * r8, last8), colb))]

    def dir_specs(d):
        return [pl.BlockSpec((1, RG_BLOCKS, RG_BS, RG_BS), lambda s: (d, 0, 0, 0)),
                pl.BlockSpec((1, 1, RG_WIDTH), lambda s: (d, 0, 0)),
                pl.BlockSpec((1, RG_BLOCKS, RG_BS, RG_BS), lambda s: (d, 0, 0, 0)),
                pl.BlockSpec((1, 1, RG_WIDTH), lambda s: (d, 0, 0)),
                pl.BlockSpec((1, 1, RG_WIDTH), lambda s: (d, 0, 0))]

    b_a3 = b_a.reshape(2, 1, RG_WIDTH)
    b_x3 = b_x.reshape(2, 1, RG_WIDTH)
    lam3 = lam.reshape(2, 1, RG_WIDTH)
    return pl.pallas_call(
        functools.partial(_rg_kernel, n_tiles=n_tiles, n_lat_tiles=n_lat_tiles),
        out_shape=(jax.ShapeDtypeStruct((t, RG_WIDTH), F32), jax.ShapeDtypeStruct((t, RG_WIDTH), F32)),
        grid=(n_tiles,),
        in_specs=tile_specs(fwd) + tile_specs(rev)
        + [pl.BlockSpec((4, RG_WIDTH), lambda s: (0, 0)), pl.BlockSpec((1, RG_WIDTH), lambda s: (0, 0))]
        + dir_specs(0) + dir_specs(1),
        out_specs=(pl.BlockSpec((ROW_TILE, RG_WIDTH), lambda s: (fwd(s), 0)),
                   pl.BlockSpec((ROW_TILE, RG_WIDTH), lambda s: (rev(s), 0))),
        scratch_shapes=[pltpu.VMEM((8, RG_WIDTH), F32), pltpu.VMEM((ROW_TILE + 16, RG_WIDTH), F32)],
        compiler_params=_cparams(("arbitrary",)),
        name="rg_lru",
    )(z, z, z, z, z, z, conv_w, conv_b, w_a, b_a3, w_x, b_x3, lam3, w_a, b_a3, w_x, b_x3, lam3)


def _assemble_kernel(of_ref, or_ref, gz_ref, on_ref, hf_ref, hr_ref, ry_ref, og_ref, org_ref):
    o = of_ref[...] + or_ref[...]
    gate = _silu(gz_ref[...])
    for h in range(GDN_HEADS):
        sl = slice(h * LANE, (h + 1) * LANE)
        seg = o[:, sl]
        on = seg * lax.rsqrt(jnp.mean(seg * seg, axis=-1, keepdims=True) + EPS) * on_ref[...]
        og_ref[:, sl] = (on * gate[:, sl]).astype(BF16)
    org_ref[...] = ((hf_ref[...] + hr_ref[...]) * _gelu_tanh(ry_ref[...])).astype(BF16)


def assemble_call(of, orv, z, o_norm, hf, hr, m):
    row = pl.BlockSpec((ROW_TILE, 1024), lambda i: (i, 0))
    return pl.pallas_call(
        _assemble_kernel,
        out_shape=(jax.ShapeDtypeStruct((m, GDN_W), BF16), jax.ShapeDtypeStruct((m, RG_WIDTH), BF16)),
        grid=(m // ROW_TILE,),
        in_specs=[row, row, pl.BlockSpec((ROW_TILE, GDN_W), lambda i: (i, Z_GZ // GDN_W)),
                  pl.BlockSpec((1, LANE), lambda i: (0, 0)), row, row,
                  pl.BlockSpec((ROW_TILE, RG_WIDTH), lambda i: (i, Z_RY // RG_WIDTH))],
        out_specs=(row, row),
        compiler_params=_cparams(("arbitrary",)),
        name="mix_assemble",
    )(of, orv, z, o_norm, hf, hr, z)


def _post_mix_kernel(y_ref, x_ref, gpost_ref, gm_ref, gpre_ref, sc_ref, sh_ref, wr_ref,
                     x1_ref, h2_ref, aff_ref, *, ctx_tile):
    is_ctx = pl.program_id(0) >= ctx_tile

    def pick(ref):
        return jnp.where(is_ctx, ref[1:2, :], ref[0:1, :])

    x1 = x_ref[...] + pick(gm_ref) * _rms(y_ref[...], gpost_ref[...])
    x1_ref[...] = x1
    h2 = _rms(x1, gpre_ref[...]) * (1.0 + pick(sc_ref)) + pick(sh_ref)
    h2b = h2.astype(BF16)
    h2r = h2b.astype(F32)
    h2_ref[...] = _pack_bf16_pair(h2r[:, 0:D_MODEL // 2], h2r[:, D_MODEL // 2:])
    h_lo = (h2 - h2b.astype(F32)).astype(BF16)
    wr = wr_ref[...]
    w_hi = wr.astype(BF16)
    w_lo = (wr - w_hi.astype(F32)).astype(BF16)
    nt = (((1,), (1,)), ((), ()))
    logits = (lax.dot_general(w_hi, h2b, nt, preferred_element_type=F32)
              + lax.dot_general(w_hi, h_lo, nt, preferred_element_type=F32)
              + lax.dot_general(w_lo, h2b, nt, preferred_element_type=F32))
    e = jnp.exp(logits - jnp.max(logits, axis=0, keepdims=True))
    aff_ref[...] = e / jnp.sum(e, axis=0, keepdims=True)


def post_mix_call(y, x, gpost, gm, gpre, sc, sh, wr_t, n_lat):
    m = y.shape[0]
    row = pl.BlockSpec((ROW_TILE, D_MODEL), lambda i: (i, 0))
    vec1 = pl.BlockSpec((1, D_MODEL), lambda i: (0, 0))
    vec2 = pl.BlockSpec((2, D_MODEL), lambda i: (0, 0))
    return pl.pallas_call(
        functools.partial(_post_mix_kernel, ctx_tile=n_lat // ROW_TILE),
        out_shape=(jax.ShapeDtypeStruct((m, D_MODEL), F32), jax.ShapeDtypeStruct((m, D_MODEL // 2), jnp.uint32),
                   jax.ShapeDtypeStruct((N_EXPERTS, m), F32)),
        grid=(m // ROW_TILE,),
        in_specs=[row, row, vec1, vec2, vec1, vec2, vec2, pl.BlockSpec((N_EXPERTS, D_MODEL), lambda i: (0, 0))],
        out_specs=(row, pl.BlockSpec((ROW_TILE, D_MODEL // 2), lambda i: (i, 0)),
                   pl.BlockSpec((N_EXPERTS, ROW_TILE), lambda i: (0, i))),
        compiler_params=_cparams(("arbitrary",)),
        name="post_mix",
    )(y, x, gpost, gm, gpre, sc, sh, wr_t)


def _post_ffn_kernel(y_ref, x_ref, g_ref, gf_ref, o_ref, *, ctx_tile):
    is_ctx = pl.program_id(0) >= ctx_tile
    gf = jnp.where(is_ctx, gf_ref[1:2, :], gf_ref[0:1, :])
    o_ref[...] = x_ref[...] + gf * _rms(y_ref[...], g_ref[...])


def post_ffn_call(y, x, gain, gf, n_lat):
    m = y.shape[0]
    row = pl.BlockSpec((ROW_TILE, D_MODEL), lambda i: (i, 0))
    return pl.pallas_call(
        functools.partial(_post_ffn_kernel, ctx_tile=n_lat // ROW_TILE),
        out_shape=jax.ShapeDtypeStruct((m, D_MODEL), F32),
        grid=(m // ROW_TILE,),
        in_specs=[row, row, pl.BlockSpec((1, D_MODEL), lambda i: (0, 0)), pl.BlockSpec((2, D_MODEL), lambda i: (0, 0))],
        out_specs=row,
        compiler_params=_cparams(("arbitrary",)),
        name="post_ffn",
    )(y, x, gain, gf)


def _pack_bf16_pair(lo, hi):
    lo_bits = lax.bitcast_convert_type(lo, jnp.uint32) >> 16
    hi_bits = lax.bitcast_convert_type(hi, jnp.uint32) & jnp.uint32(0xFFFF0000)
    return lo_bits | hi_bits


def _unpack_bf16_pair(w):
    lo = lax.bitcast_convert_type(w << 16, F32).astype(BF16)
    hi = lax.bitcast_convert_type(w & jnp.uint32(0xFFFF0000), F32).astype(BF16)
    return lo, hi


def _moe_up_kernel(idx_ref, h_hbm, wg_ref, wu_ref, o_ref, xbuf, sem, *, cap, n_ff_tiles):
    e = pl.program_id(0)
    n_e = pl.num_programs(0)
    half = D_MODEL // 2

    f = pl.program_id(1)
    n_f = pl.num_programs(1)
    slot = e % 2
    part = cap // n_ff_tiles

    def row_copy(expert, s, dst_slot):
        row = idx_ref[expert, s]
        return pltpu.make_async_copy(h_hbm.at[pl.ds(row, 1), :], xbuf.at[dst_slot, pl.ds(s, 1), :], sem.at[dst_slot])

    def wait_rows(dst_slot):
        pltpu.make_async_copy(h_hbm.at[pl.ds(0, cap), :], xbuf.at[dst_slot], sem.at[dst_slot]).wait()

    @pl.when(f == 0)
    def _():
        @pl.when(e == 0)
        def _():
            def body(s, carry):
                row_copy(0, s, 0).start()
                return carry
            lax.fori_loop(0, cap, body, 0)

        wait_rows(slot)

    nxt = jnp.minimum(e + 1, n_e - 1)
    for j in range(part):
        row_copy(nxt, f * part + j, 1 - slot).start()

    x_lo, x_hi = _unpack_bf16_pair(xbuf[slot])

    def proj(w_ref):
        return (jnp.dot(x_lo, w_ref[0, 0, 0:half, :].astype(BF16), preferred_element_type=F32)
                + jnp.dot(x_hi, w_ref[0, 0, half:, :].astype(BF16), preferred_element_type=F32))

    o_ref[0] = (_silu(proj(wg_ref)) * proj(wu_ref)).astype(BF16)

    @pl.when(jnp.logical_and(e == n_e - 1, f == n_f - 1))
    def _():
        wait_rows(1 - slot)


def moe_up_call(idx, h_packed, w_gate, w_up, l, tf=256):
    e, cap = idx.shape
    d = D_MODEL
    return pl.pallas_call(
        functools.partial(_moe_up_kernel, cap=cap, n_ff_tiles=EXPERT_FF // tf),
        out_shape=jax.ShapeDtypeStruct((e, cap, EXPERT_FF), BF16),
        grid_spec=pltpu.PrefetchScalarGridSpec(
            num_scalar_prefetch=1,
            grid=(e, EXPERT_FF // tf),
            in_specs=[pl.BlockSpec(memory_space=pl.ANY),
                      pl.BlockSpec((1, 1, d, tf), lambda i, f, idx_ref: (l, i, 0, f)),
                      pl.BlockSpec((1, 1, d, tf), lambda i, f, idx_ref: (l, i, 0, f))],
            out_specs=pl.BlockSpec((1, cap, tf), lambda i, f, idx_ref: (i, 0, f)),
            scratch_shapes=[pltpu.VMEM((2, cap, d // 2), jnp.uint32), pltpu.SemaphoreType.DMA((2,))]),
        compiler_params=_cparams(("arbitrary", "arbitrary")),
        name="moe_up",
    )(idx, h_packed, w_gate, w_up)


def _moe_scatter_kernel(idx_ref, y_ref, acc_in, acc_hbm, buf, sem_in, sem_out, *, rows):
    del acc_in
    e = pl.program_id(0)
    base = pl.program_id(1) * rows

    def gather(s, carry):
        row = idx_ref[e, base + s]
        pltpu.make_async_copy(acc_hbm.at[pl.ds(row, 1), :], buf.at[pl.ds(s, 1), :], sem_in).start()
        return carry

    def scatter(s, carry):
        row = idx_ref[e, base + s]
        pltpu.make_async_copy(buf.at[pl.ds(s, 1), :], acc_hbm.at[pl.ds(row, 1), :], sem_out).start()
        return carry

    lax.fori_loop(0, rows, gather, 0, unroll=8)
    pltpu.make_async_copy(acc_hbm.at[pl.ds(0, rows), :], buf, sem_in).wait()
    buf[...] = buf[...] + y_ref[0]
    lax.fori_loop(0, rows, scatter, 0, unroll=8)
    pltpu.make_async_copy(buf, acc_hbm.at[pl.ds(0, rows), :], sem_out).wait()


def moe_scatter_call(idx, y_e, m):
    e, cap, d = y_e.shape
    rows = cap // 2
    return pl.pallas_call(
        functools.partial(_moe_scatter_kernel, rows=rows),
        out_shape=jax.ShapeDtypeStruct((m, d), F32),
        grid_spec=pltpu.PrefetchScalarGridSpec(
            num_scalar_prefetch=1,
            grid=(e, 2),
            in_specs=[pl.BlockSpec((1, rows, d), lambda i, hf, idx_ref: (i, hf, 0)),
                      pl.BlockSpec(memory_space=pl.ANY)],
            out_specs=pl.BlockSpec(memory_space=pl.ANY),
            scratch_shapes=[pltpu.VMEM((rows, d), F32), pltpu.SemaphoreType.DMA(()), pltpu.SemaphoreType.DMA(())]),
        input_output_aliases={2: 0},
        compiler_params=_cparams(("arbitrary", "arbitrary")),
        name="moe_scatter",
    )(idx, y_e, jnp.zeros((m, d), F32))


def _moe_down_kernel(h_ref, wd_ref, g_ref, o_ref):
    y = jnp.dot(h_ref[0], wd_ref[0, 0].astype(BF16), preferred_element_type=F32)
    o_ref[0] = y * g_ref[0]


def moe_down_call(hid, w_down, l, gates, tn=512):
    e, c, f = hid.shape
    d = w_down.shape[3]
    return pl.pallas_call(
        _moe_down_kernel,
        out_shape=jax.ShapeDtypeStruct((e, c, d), F32),
        grid=(e, d // tn),
        in_specs=[pl.BlockSpec((1, c, f), lambda i, n: (i, 0, 0)),
                  pl.BlockSpec((1, 1, f, tn), lambda i, n: (l, i, 0, n)),
                  pl.BlockSpec((1, c, 1), lambda i, n: (i, 0, 0))],
        out_specs=pl.BlockSpec((1, c, tn), lambda i, n: (i, 0, n)),
        compiler_params=_cparams(("arbitrary", "arbitrary")),
        name="moe_down",
    )(hid, w_down, gates)


def _rope_tables(n_lat, n_ctx):
    rows = n_lat // GRID_W
    row = jnp.broadcast_to(jnp.arange(rows, dtype=F32)[:, None], (rows, GRID_W)).reshape(-1)
    col = jnp.broadcast_to(jnp.arange(GRID_W, dtype=F32)[None, :], (rows, GRID_W)).reshape(-1)
    n_freq = MLA_ROPE // 4
    inv_freq = ROPE_THETA ** (-jnp.arange(n_freq, dtype=F32) / n_freq)
    ang = jnp.concatenate([row[:, None] * inv_freq, col[:, None] * inv_freq], axis=-1)
    cos = jnp.cos(ang)
    sin = jnp.sin(ang)
    cos64 = jnp.concatenate([cos, cos], axis=-1)
    sin64 = jnp.concatenate([sin, sin], axis=-1)
    one_l = jnp.ones((n_lat, 64), F32)
    zero_l = jnp.zeros((n_lat, 64), F32)
    one_c = jnp.ones((n_ctx, 64), F32)
    zero_c = jnp.zeros((n_ctx, 64), F32)
    cq = jnp.concatenate([jnp.concatenate([cos64, one_l], 1), jnp.concatenate([one_c, one_c], 1)], 0)
    sq = jnp.concatenate([jnp.concatenate([sin64, zero_l], 1), jnp.concatenate([zero_c, zero_c], 1)], 0)
    ck = jnp.concatenate([jnp.concatenate([cos64, zero_l], 1), jnp.concatenate([zero_c, one_c], 1)], 0)
    sk = jnp.concatenate([jnp.concatenate([sin64, zero_l], 1), jnp.concatenate([zero_c, zero_c], 1)], 0)
    return cq, sq, ck, sk


def _rot_half_cols(w):
    half = MLA_ROPE // 2
    return jnp.concatenate([-w[..., half:], w[..., :half]], axis=-1)


def _relayout_w_in(w_in):
    q_a = w_in[:, 0:1024]
    kv_a = w_in[:, 1024:1536]
    k_pe = w_in[:, 1536:1600]
    g_qkv = w_in[:, 1600:4672]
    g_z = w_in[:, 4672:5696]
    g_ba = w_in[:, 5696:5728]
    r_x = w_in[:, 5728:6752]
    r_y = w_in[:, 6752:7776]
    zeros = functools.partial(jnp.zeros, dtype=w_in.dtype)
    d = w_in.shape[0]
    return jnp.concatenate([q_a, g_qkv, g_z, r_x, r_y, kv_a, k_pe, k_pe, _rot_half_cols(k_pe), zeros((d, 64)),
                            g_ba, zeros((d, LANE - 32)), zeros((d, LANE))], axis=1)


def _relayout_w_qb(w_qb):
    w = w_qb.reshape(MLA_Q_RANK, MLA_HEADS, MLA_NOPE + MLA_ROPE)
    nope = w[:, :, :MLA_NOPE]
    pe = w[:, :, MLA_NOPE:]
    zeros = jnp.zeros_like(pe)
    out = jnp.concatenate([nope, pe, pe, _rot_half_cols(pe), zeros], axis=-1)
    return jnp.transpose(out, (1, 0, 2))


BIG_WEIGHTS = ('mod_w', 'mla_w_kvb', 'w_out', 'w_gate', 'w_up', 'w_down')


def _layer(x_all, c_rows, p, big, l, n_lat, n_ctx, update_ctx):
    t = n_lat + n_ctx
    mod = mod_call(c_rows, big['mod_w'], p['mod_b'][None, :], l)[0:2]
    sh_m, sc_m, g_m, sh_f, sc_f, g_f = [mod[:, i * D_MODEL:(i + 1) * D_MODEL] for i in range(6)]

    h = prenorm_call(x_all, p['norm_mix_pre'][None, :], sc_m, sh_m, n_lat)
    z = matmul_call(h, _relayout_w_in(p['w_in']), name="w_in")

    cq, sq, ck, sk = _rope_tables(n_lat, n_ctx)
    q = qproj_call(z, p['mla_q_norm'][None, :], _relayout_w_qb(p['mla_w_qb']), cq, sq)
    tk = next(c for c in (768, 256) if t % c == 0)
    k, v = kvproj_call(z, p['mla_kv_norm'][None, :], big['mla_w_kvb'], l, ck, sk, tk)
    o_mla = attn_call(q, k, v, n_lat, 0, t, 0, tq=512, tk=tk)
    if update_ctx:
        o_mla_c = attn_call(q, k, v, n_ctx, n_lat, n_ctx, n_lat, tq=n_ctx, tk=n_ctx)
        o_mla = jnp.concatenate([o_mla, o_mla_c], axis=0)

    pad = jnp.zeros((2 * GDN_HEADS,), F32)
    alog_row = jnp.concatenate([pad, p['gdn_a_log'].reshape(-1), jnp.zeros((LANE - 4 * GDN_HEADS,), F32)])[None, :]
    dtb_row = jnp.concatenate([pad, p['gdn_dt_bias'].reshape(-1), jnp.zeros((LANE - 4 * GDN_HEADS,), F32)])[None, :]
    qkv, bg = gdn_prep_call(z, p['gdn_conv_w'], alog_row, dtb_row, n_lat)
    gt = jnp.swapaxes(bg[:, 2 * GDN_HEADS:4 * GDN_HEADS].reshape(t // GDN_CHUNK, GDN_CHUNK, 2 * GDN_HEADS), 1, 2)
    o_f, o_r = gdn_call(qkv, bg, gt, n_lat)

    h_f, h_r = rg_call(z, p['rg_conv_w'], p['rg_conv_b'][None, :], p['rg_w_a'], p['rg_b_a'], p['rg_w_x'],
                       p['rg_b_x'], p['rg_lambda'], n_lat)

    m = t if update_ctx else n_lat
    o_gdn, o_rg = assemble_call(o_f, o_r, z, p['gdn_o_norm'][None, :], h_f, h_r, m)
    y = matmul3_call(o_mla, o_gdn, o_rg, big['w_out'], l, m)
    x1, h2, aff_t = post_mix_call(y, x_all, p['norm_mix_post'][None, :], g_m, p['norm_ffn_pre'][None, :],
                                  sc_f, sh_f, p['w_router'].T, n_lat)

    cap = CAPACITY_FACTOR * n_lat // N_EXPERTS
    gates, idx = lax.top_k(aff_t[:, :n_lat], cap)
    if update_ctx:
        cap_c = CAPACITY_FACTOR * n_ctx // N_EXPERTS
        gates_c, idx_c = lax.top_k(aff_t[:, n_lat:], cap_c)
        gates = jnp.concatenate([gates, gates_c], axis=1)
        idx = jnp.concatenate([idx, idx_c + n_lat], axis=1)
    hid = moe_up_call(idx, h2, big['w_gate'], big['w_up'], l)
    y_e = moe_down_call(hid, big['w_down'], l, gates[..., None])
    y_moe = moe_scatter_call(idx, y_e, m)
    return post_ffn_call(y_moe, x1, p['norm_ffn_post'][None, :], g_f, n_lat)


def kernel(x, c, ctx, c_ctx, mod_w, mod_b, norm_mix_pre, norm_mix_post, norm_ffn_pre, norm_ffn_post,
           w_in, mla_q_norm, mla_kv_norm, mla_w_qb, mla_w_kvb, gdn_conv_w, gdn_a_log, gdn_dt_bias,
           gdn_o_norm, rg_conv_w, rg_conv_b, rg_w_a, rg_b_a, rg_w_x, rg_b_x, rg_lambda, w_out,
           w_router, w_gate, w_up, w_down):
    assert x.shape[0] == 1 and ctx.shape[0] == 1
    n_lat, n_ctx = x.shape[1], ctx.shape[1]
    assert n_ctx == ROW_TILE and n_lat % 512 == 0
    depth = mod_w.shape[0]
    stacked = dict(mod_w=mod_w, mod_b=mod_b, norm_mix_pre=norm_mix_pre, norm_mix_post=norm_mix_post,
                   norm_ffn_pre=norm_ffn_pre, norm_ffn_post=norm_ffn_post, w_in=w_in, mla_q_norm=mla_q_norm,
                   mla_kv_norm=mla_kv_norm, mla_w_qb=mla_w_qb, mla_w_kvb=mla_w_kvb, gdn_conv_w=gdn_conv_w,
                   gdn_a_log=gdn_a_log, gdn_dt_bias=gdn_dt_bias, gdn_o_norm=gdn_o_norm, rg_conv_w=rg_conv_w,
                   rg_conv_b=rg_conv_b, rg_w_a=rg_w_a, rg_b_a=rg_b_a, rg_w_x=rg_w_x, rg_b_x=rg_b_x,
                   rg_lambda=rg_lambda, w_out=w_out, w_router=w_router, w_gate=w_gate, w_up=w_up, w_down=w_down)
    c_rows = jnp.concatenate([c, c_ctx[None, :], jnp.zeros((6, D_MODEL), F32)], axis=0)
    x_all = jnp.concatenate([x[0], ctx[0]], axis=0)
    big = {k: stacked[k] for k in BIG_WEIGHTS}
    for l in range(depth):
        p = {k: v[l] for k, v in stacked.items() if k not in BIG_WEIGHTS}
        x_all = _layer(x_all, c_rows, p, big, l, n_lat, n_ctx, l < depth - 1)
    return x_all[None]
```

```python
import functools
import math

import jax
import jax.numpy as jnp
from jax import lax
from jax.experimental import pallas as pl
from jax.experimental.pallas import tpu as pltpu

F32 = jnp.float32
BF16 = jnp.bfloat16

D_MODEL = 4096
EPS = 1e-6
GRID_W = 64
ROPE_THETA = 10000.0

MLA_HEADS = 16
MLA_NOPE = 128
MLA_ROPE = 64
MLA_V = 128
MLA_Q_RANK = 1024
MLA_KV_RANK = 512
VT_ROWS = MLA_V + 16

GDN_HEADS = 8
GDN_DK = 128
GDN_DV = 128
GDN_CHUNK = 64
GDN_W = GDN_HEADS * GDN_DK

RG_WIDTH = 1024
RG_BLOCKS = 8
RG_BS = RG_WIDTH // RG_BLOCKS
RG_C = 8.0

N_EXPERTS = 16
EXPERT_FF = 1024
CAPACITY_FACTOR = 2

LANE = 128
SUBLANES = 8
ROW_TILE = 256
VMEM_LIMIT = 56 * 1024 * 1024

Z_QA, Z_GQ, Z_GK, Z_GV, Z_GZ, Z_RX, Z_RY = 0, 1024, 2048, 3072, 4096, 5120, 6144
Z_KVA = 7168
Z_PEA, Z_PEB, Z_BA = 7680, 7808, 7936
Z_COLS = 8192


def _cparams(sem, vmem=VMEM_LIMIT):
    return pltpu.CompilerParams(dimension_semantics=sem, vmem_limit_bytes=vmem)


def _sigmoid(x):
    return 1.0 / (1.0 + jnp.exp(-x))


def _silu(x):
    return x * _sigmoid(x)


def _softplus(x):
    return jnp.maximum(x, 0.0) + jnp.log(1.0 + jnp.exp(-jnp.abs(x)))


def _one_minus_exp(y):
    poly = 1.0 / math.factorial(8)
    for j in range(6, -1, -1):
        poly = poly * y + 1.0 / math.factorial(j + 1)
    return jnp.where(y > -0.125, -(y * poly), 1.0 - jnp.exp(y))


def _gelu_tanh(x):
    return 0.5 * x * (1.0 + jnp.tanh(math.sqrt(2.0 / math.pi) * (x + 0.044715 * (x * x * x))))


def _rms(x, gain):
    return x * lax.rsqrt(jnp.mean(x * x, axis=-1, keepdims=True) + EPS) * gain


def _mod_kernel(c_ref, w_ref, b_ref, o_ref):
    s = _silu(c_ref[...])
    o_ref[...] = jnp.dot(s.astype(BF16), w_ref[0].astype(BF16), preferred_element_type=F32) + b_ref[...]


def mod_call(cc, w, b, l):
    tn = 1024
    n = w.shape[2]
    return pl.pallas_call(
        _mod_kernel,
        out_shape=jax.ShapeDtypeStruct((8, n), F32),
        grid=(n // tn,),
        in_specs=[pl.BlockSpec((8, D_MODEL), lambda j: (0, 0)),
                  pl.BlockSpec((1, D_MODEL, tn), lambda j: (l, 0, j)),
                  pl.BlockSpec((1, tn), lambda j: (0, j))],
        out_specs=pl.BlockSpec((8, tn), lambda j: (0, j)),
        compiler_params=_cparams(("arbitrary",)),
        name="mod",
    )(cc, w, b)


def _prenorm_kernel(x_ref, g_ref, sc_ref, sh_ref, o_ref, *, ctx_tile):
    is_ctx = pl.program_id(0) >= ctx_tile
    sc = jnp.where(is_ctx, sc_ref[1:2, :], sc_ref[0:1, :])
    sh = jnp.where(is_ctx, sh_ref[1:2, :], sh_ref[0:1, :])
    y = _rms(x_ref[...], g_ref[...])
    o_ref[...] = (y * (1.0 + sc) + sh).astype(BF16)


def prenorm_call(x, gain, sc, sh, n_lat):
    t = x.shape[0]
    row = pl.BlockSpec((ROW_TILE, D_MODEL), lambda i: (i, 0))
    return pl.pallas_call(
        functools.partial(_prenorm_kernel, ctx_tile=n_lat // ROW_TILE),
        out_shape=jax.ShapeDtypeStruct((t, D_MODEL), BF16),
        grid=(t // ROW_TILE,),
        in_specs=[row, pl.BlockSpec((1, D_MODEL), lambda i: (0, 0)),
                  pl.BlockSpec((2, D_MODEL), lambda i: (0, 0)), pl.BlockSpec((2, D_MODEL), lambda i: (0, 0))],
        out_specs=row,
        compiler_params=_cparams(("arbitrary",)),
        name="prenorm",
    )(x, gain, sc, sh)


def _matmul_kernel(a_ref, w_ref, o_ref):
    o_ref[...] = jnp.dot(a_ref[...], w_ref[...].astype(BF16), preferred_element_type=F32)


def _pick_tm(m):
    for tm in (1056, 1024, 768, 512, 256):
        if m % tm == 0:
            return tm
    raise ValueError(f"unsupported row count {m}")


def matmul_call(a, w, tn=512, name="matmul"):
    m, k = a.shape
    n = w.shape[1]
    tm = _pick_tm(m)
    return pl.pallas_call(
        _matmul_kernel,
        out_shape=jax.ShapeDtypeStruct((m, n), F32),
        grid=(m // tm, n // tn),
        in_specs=[pl.BlockSpec((tm, k), lambda i, j: (i, 0)), pl.BlockSpec((k, tn), lambda i, j: (0, j))],
        out_specs=pl.BlockSpec((tm, tn), lambda i, j: (i, j)),
        compiler_params=_cparams(("arbitrary", "arbitrary")),
        name=name,
    )(a, w)


def _matmul3_kernel(a1_ref, a2_ref, a3_ref, w_ref, o_ref):
    k1 = a1_ref.shape[1]
    k2 = a2_ref.shape[1]
    acc = jnp.dot(a1_ref[...], w_ref[0, 0:k1, :].astype(BF16), preferred_element_type=F32)
    acc += jnp.dot(a2_ref[...], w_ref[0, k1:k1 + k2, :].astype(BF16), preferred_element_type=F32)
    acc += jnp.dot(a3_ref[...], w_ref[0, k1 + k2:, :].astype(BF16), preferred_element_type=F32)
    o_ref[...] = acc


def matmul3_call(a1, a2, a3, w, l, m, tn=512):
    k = w.shape[1]
    n = w.shape[2]
    tm = _pick_tm(m)
    return pl.pallas_call(
        _matmul3_kernel,
        out_shape=jax.ShapeDtypeStruct((m, n), F32),
        grid=(m // tm, n // tn),
        in_specs=[pl.BlockSpec((tm, a1.shape[1]), lambda i, j: (i, 0)),
                  pl.BlockSpec((tm, a2.shape[1]), lambda i, j: (i, 0)),
                  pl.BlockSpec((tm, a3.shape[1]), lambda i, j: (i, 0)),
                  pl.BlockSpec((1, k, tn), lambda i, j: (l, 0, j))],
        out_specs=pl.BlockSpec((tm, tn), lambda i, j: (i, j)),
        compiler_params=_cparams(("arbitrary", "arbitrary")),
        name="w_out",
    )(a1, a2, a3, w)


Q_HEAD_GROUP = 4


def _qproj_kernel(z_ref, g_ref, w_ref, c_ref, s_ref, o_ref, zn_ref, *, scale):
    @pl.when(pl.program_id(1) == 0)
    def _():
        zn_ref[...] = _rms(z_ref[...], g_ref[...]).astype(BF16)

    for g in range(Q_HEAD_GROUP):
        r = jnp.dot(zn_ref[...], w_ref[g].astype(BF16), preferred_element_type=F32)
        q0 = r[:, 0:LANE]
        q1 = r[:, LANE:2 * LANE] * c_ref[...] + r[:, 2 * LANE:3 * LANE] * s_ref[...]
        o_ref[g, :, 0:LANE] = (q0 * scale).astype(BF16)
        o_ref[g, :, LANE:2 * LANE] = (q1 * scale).astype(BF16)


def qproj_call(z, q_norm, wq, cq, sq):
    t = z.shape[0]
    tm = _pick_tm(t)
    scale = (MLA_NOPE + MLA_ROPE) ** -0.5 * math.log2(math.e)
    return pl.pallas_call(
        functools.partial(_qproj_kernel, scale=scale),
        out_shape=jax.ShapeDtypeStruct((MLA_HEADS, t, 2 * LANE), BF16),
        grid=(t // tm, MLA_HEADS // Q_HEAD_GROUP),
        in_specs=[pl.BlockSpec((tm, MLA_Q_RANK), lambda i, h: (i, Z_QA // MLA_Q_RANK)),
                  pl.BlockSpec((1, MLA_Q_RANK), lambda i, h: (0, 0)),
                  pl.BlockSpec((Q_HEAD_GROUP, MLA_Q_RANK, 3 * LANE), lambda i, h: (h, 0, 0)),
                  pl.BlockSpec((tm, LANE), lambda i, h: (i, 0)),
                  pl.BlockSpec((tm, LANE), lambda i, h: (i, 0))],
        out_specs=pl.BlockSpec((Q_HEAD_GROUP, tm, 2 * LANE), lambda i, h: (h, i, 0)),
        scratch_shapes=[pltpu.VMEM((tm, MLA_Q_RANK), BF16)],
        compiler_params=_cparams(("arbitrary", "arbitrary")),
        name="q_proj",
    )(z, q_norm, wq, cq, sq)


KV_HEAD_GROUP = 4


def _kvproj_kernel(z_ref, g_ref, w_ref, pa_ref, pb_ref, c_ref, s_ref, k_ref, v_ref, zn_ref):
    @pl.when(pl.program_id(1) == 0)
    def _():
        zn_ref[...] = _rms(z_ref[...], g_ref[...]).astype(BF16)

    r = jnp.dot(zn_ref[...], w_ref[0].astype(BF16), preferred_element_type=F32)
    k1 = (pa_ref[...] * c_ref[...] + pb_ref[...] * s_ref[...]).astype(BF16)
    for g in range(KV_HEAD_GROUP):
        c0 = g * 2 * LANE
        k_ref[g, :, 0:LANE] = r[:, c0:c0 + LANE].astype(BF16)
        k_ref[g, :, LANE:2 * LANE] = k1
        v_ref[g, 0, 0:MLA_V, :] = r[:, c0 + LANE:c0 + 2 * LANE].T.astype(BF16)
        v_ref[g, 0, MLA_V:VT_ROWS, :] = jnp.ones((VT_ROWS - MLA_V, r.shape[0]), BF16)


def kvproj_call(z, kv_norm, wkv, l, ck, sk, tm):
    t = z.shape[0]
    return pl.pallas_call(
        _kvproj_kernel,
        out_shape=(jax.ShapeDtypeStruct((MLA_HEADS, t, 2 * LANE), BF16),
                   jax.ShapeDtypeStruct((MLA_HEADS, t // tm, VT_ROWS, tm), BF16)),
        grid=(t // tm, MLA_HEADS // KV_HEAD_GROUP),
        in_specs=[pl.BlockSpec((tm, MLA_KV_RANK), lambda i, h: (i, Z_KVA // MLA_KV_RANK)),
                  pl.BlockSpec((1, MLA_KV_RANK), lambda i, h: (0, 0)),
                  pl.BlockSpec((1, MLA_KV_RANK, KV_HEAD_GROUP * 2 * LANE), lambda i, h: (l, 0, h)),
                  pl.BlockSpec((tm, LANE), lambda i, h: (i, Z_PEA // LANE)),
                  pl.BlockSpec((tm, LANE), lambda i, h: (i, Z_PEB // LANE)),
                  pl.BlockSpec((tm, LANE), lambda i, h: (i, 0)),
                  pl.BlockSpec((tm, LANE), lambda i, h: (i, 0))],
        out_specs=(pl.BlockSpec((KV_HEAD_GROUP, tm, 2 * LANE), lambda i, h: (h, i, 0)),
                   pl.BlockSpec((KV_HEAD_GROUP, 1, VT_ROWS, tm), lambda i, h: (h, i, 0, 0))),
        scratch_shapes=[pltpu.VMEM((tm, MLA_KV_RANK), BF16)],
        compiler_params=_cparams(("arbitrary", "arbitrary")),
        name="kv_proj",
    )(z, kv_norm, wkv, z, z, ck, sk)


ATTN_LOOKAHEAD = 2


def _attn_kernel(q_ref, k_ref, vt_ref, o_ref, m_ref, acc_ref, *bufs, tk, n_chunks):
    q = q_ref[0]
    nt = (((1,), (1,)), ((), ()))

    def scores(c):
        return lax.dot_general(k_ref[0, c * tk:(c + 1) * tk, :], q, nt, preferred_element_type=F32)

    def consume(c, s_ref):
        m_old = m_ref[...]
        m_new = jnp.maximum(m_old, jnp.max(s_ref[...], axis=0, keepdims=True))
        alpha = jnp.exp2(m_old - m_new)
        p = jnp.exp2(s_ref[...] - m_new)
        pv = jnp.dot(vt_ref[0, c], p.astype(BF16), preferred_element_type=F32)
        acc_ref[...] = alpha * acc_ref[...] + pv
        m_ref[...] = m_new

    m_ref[...] = jnp.full(m_ref.shape, -jnp.inf, F32)
    acc_ref[...] = jnp.zeros(acc_ref.shape, F32)
    for c in range(min(ATTN_LOOKAHEAD, n_chunks)):
        bufs[c][...] = scores(c)
    for c in range(n_chunks):
        if c + ATTN_LOOKAHEAD < n_chunks:
            bufs[(c + ATTN_LOOKAHEAD) % len(bufs)][...] = scores(c + ATTN_LOOKAHEAD)
        consume(c, bufs[c % len(bufs)])
    o_ref[...] = (acc_ref[0:MLA_V, :] / acc_ref[MLA_V:MLA_V + 1, :]).T.astype(BF16)


def attn_call(q, k, vt, n_q, q_row0, n_k, k_row0, tq, tk):
    ck = vt.shape[3]
    assert n_q % tq == 0 and q_row0 % tq == 0 and n_k % tk == 0 and k_row0 % n_k == 0
    n_chunks = n_k // tk
    if n_chunks > 1:
        assert tk == ck and k_row0 == 0
        vt_spec = pl.BlockSpec((1, n_chunks, VT_ROWS, ck), lambda h, j: (h, 0, 0, 0))
    else:
        assert (k_row0 % ck) % tk == 0
        vt_spec = pl.BlockSpec((1, 1, VT_ROWS, tk), lambda h, j: (h, k_row0 // ck, 0, (k_row0 % ck) // tk))
    qb, kb = q_row0 // tq, k_row0 // n_k
    return pl.pallas_call(
        functools.partial(_attn_kernel, tk=tk, n_chunks=n_chunks),
        out_shape=jax.ShapeDtypeStruct((n_q, MLA_HEADS * MLA_V), BF16),
        grid=(MLA_HEADS, n_q // tq),
        in_specs=[pl.BlockSpec((1, tq, 2 * LANE), lambda h, j: (h, j + qb, 0)),
                  pl.BlockSpec((1, n_k, 2 * LANE), lambda h, j: (h, kb, 0)),
                  vt_spec],
        out_specs=pl.BlockSpec((tq, MLA_V), lambda h, j: (j, h)),
        scratch_shapes=[pltpu.VMEM((1, tq), F32), pltpu.VMEM((VT_ROWS, tq), F32)]
        + [pltpu.VMEM((tk, tq), F32)] * (ATTN_LOOKAHEAD + 1),
        compiler_params=_cparams(("arbitrary", "arbitrary")),
        name="mla_attn",
    )(q, k, vt)


def _conv4(x_ref, p_ref, n_ref, w, ext_ref, has_prev, has_next):
    t = x_ref.shape[0]
    ext_ref[0:8, :] = jnp.where(has_prev, p_ref[...], 0.0)
    ext_ref[8:8 + t, :] = x_ref[...]
    ext_ref[8 + t:16 + t, :] = jnp.where(has_next, n_ref[...], 0.0)
    y = ext_ref[6:6 + t, :] * w[0:1, :]
    y += ext_ref[7:7 + t, :] * w[1:2, :]
    y += ext_ref[8:8 + t, :] * w[2:3, :]
    y += ext_ref[9:9 + t, :] * w[3:4, :]
    return y


def _halo_flags(i, n_lat_tiles):
    has_prev = jnp.logical_and(i > 0, i < n_lat_tiles)
    has_next = i < n_lat_tiles - 1
    return has_prev, has_next


def _halo_specs(width, col_of, n_tiles):
    r8 = ROW_TILE // 8
    last8 = n_tiles * r8 - 1
    main = pl.BlockSpec((ROW_TILE, width), lambda i, *a: (i, col_of(*a)))
    prev = pl.BlockSpec((8, width), lambda i, *a: (jnp.maximum(i * r8 - 1, 0), col_of(*a)))
    nxt = pl.BlockSpec((8, width), lambda i, *a: (jnp.minimum((i + 1) * r8, last8), col_of(*a)))
    return main, prev, nxt


def _gdn_prep_kernel(x_ref, p_ref, n_ref, w_ref, ba_ref, alog_ref, dtb_ref, o_ref, bg_ref, ext_ref, *, n_lat_tiles):
    i = pl.program_id(0)
    j = pl.program_id(1)
    has_prev, has_next = _halo_flags(i, n_lat_tiles)
    y = _silu(_conv4(x_ref, p_ref, n_ref, w_ref[...], ext_ref, has_prev, has_next))
    qscale = jnp.where(j == 0, GDN_DK ** -0.5, 1.0)
    is_v = j == 2
    for h in range(GDN_HEADS):
        seg = y[:, h * LANE:(h + 1) * LANE]
        nrm = seg * lax.rsqrt(jnp.sum(seg * seg, axis=-1, keepdims=True) + EPS) * qscale
        o_ref[0, :, h * LANE:(h + 1) * LANE] = jnp.where(is_v, seg, nrm)

    @pl.when(j == 0)
    def _():
        ba = ba_ref[...]
        lane = lax.broadcasted_iota(jnp.int32, ba.shape, 1)
        beta = _sigmoid(ba)
        g = -jnp.exp(alog_ref[...]) * _softplus(ba + dtb_ref[...])
        bg_ref[...] = jnp.where(lane < 2 * GDN_HEADS, beta, g)


def gdn_prep_call(z, conv_w, alog_row, dtb_row, n_lat):
    t = z.shape[0]
    n_tiles = t // ROW_TILE
    main, prev, nxt = _halo_specs(GDN_W, lambda j: Z_GQ // GDN_W + j, n_tiles)
    row128 = pl.BlockSpec((1, LANE), lambda i, j: (0, 0))
    return pl.pallas_call(
        functools.partial(_gdn_prep_kernel, n_lat_tiles=n_lat // ROW_TILE),
        out_shape=(jax.ShapeDtypeStruct((3, t, GDN_W), F32), jax.ShapeDtypeStruct((t, LANE), F32)),
        grid=(n_tiles, 3),
        in_specs=[main, prev, nxt,
                  pl.BlockSpec((4, GDN_W), lambda i, j: (0, j)),
                  pl.BlockSpec((ROW_TILE, LANE), lambda i, j: (i, Z_BA // LANE)),
                  row128, row128],
        out_specs=(pl.BlockSpec((1, ROW_TILE, GDN_W), lambda i, j: (j, i, 0)),
                   pl.BlockSpec((ROW_TILE, LANE), lambda i, j: (i, 0))),
        scratch_shapes=[pltpu.VMEM((ROW_TILE + 16, GDN_W), F32)],
        compiler_params=_cparams(("arbitrary", "arbitrary")),
        name="gdn_prep",
    )(z, z, z, conv_w, z, alog_row, dtb_row)


def _split3(x):
    hi = x.astype(BF16)
    r1 = x - hi.astype(F32)
    mid = r1.astype(BF16)
    lo = (r1 - mid.astype(F32)).astype(BF16)
    return hi, mid, lo


def _bdot(a, b):
    return jnp.dot(a.astype(BF16), b.astype(BF16), preferred_element_type=F32)


def _gdn_gates(q_ref, k_ref, v_ref, bg_ref, gt_ref, o_ref, d):
    c = GDN_CHUNK
    row = lax.broadcasted_iota(jnp.int32, (c, c), 0)
    col = lax.broadcasted_iota(jnp.int32, (c, c), 1)
    if d == 0:
        incl = row >= col
        strict = row > col
    else:
        incl = row <= col
        strict = row < col
    incl_t = (row <= col) if d == 0 else (row >= col)
    tri = jnp.where(incl, 1.0, 0.0).astype(BF16)
    tri_t = jnp.where(incl_t, 1.0, 0.0).astype(BF16)
    bg = bg_ref[...]
    g_hi, g_mid, g_lo = _split3(bg)
    gc_all = (jnp.dot(tri, g_hi, preferred_element_type=F32) + jnp.dot(tri, g_mid, preferred_element_type=F32)
              + jnp.dot(tri, g_lo, preferred_element_type=F32))
    t_hi, t_mid, t_lo = _split3(gt_ref[0])
    gr_all = (jnp.dot(t_hi, tri_t, preferred_element_type=F32) + jnp.dot(t_mid, tri_t, preferred_element_type=F32)
              + jnp.dot(t_lo, tri_t, preferred_element_type=F32))
    last = c - 1 if d == 0 else 0
    heads = []
    for h in range(GDN_HEADS):
        vh = d * GDN_HEADS + h
        sl = slice(h * LANE, (h + 1) * LANE)
        gcol = gc_all[:, 2 * GDN_HEADS + vh:2 * GDN_HEADS + vh + 1]
        heads.append(dict(
            vh=vh, sl=sl, o_ref=o_ref, incl=incl, strict=strict,
            qh=q_ref[0, :, sl], kh=k_ref[0, :, sl], vv=v_ref[0, :, sl],
            beta=bg[:, vh:vh + 1],
            gcol=gcol,
            grow=gr_all[vh:vh + 1, :],
            g_last=gcol[last:last + 1, :]))
    return heads


def _gdn_kernel(qf_ref, kf_ref, vf_ref, bgf_ref, gtf_ref, qr_ref, kr_ref, vr_ref, bgr_ref, gtr_ref,
                of_ref, or_ref, s_ref):
    @pl.when(pl.program_id(0) == 0)
    def _():
        s_ref[...] = jnp.zeros(s_ref.shape, F32)

    c = GDN_CHUNK
    hs = (_gdn_gates(qf_ref, kf_ref, vf_ref, bgf_ref, gtf_ref, of_ref, 0)
          + _gdn_gates(qr_ref, kr_ref, vr_ref, bgr_ref, gtr_ref, or_ref, 1))
    row = lax.broadcasted_iota(jnp.int32, (c, c), 0)
    col = lax.broadcasted_iota(jnp.int32, (c, c), 1)
    eye = jnp.where(row == col, 1.0, 0.0)
    nt = (((1,), (1,)), ((), ()))
    for t in hs:
        kb = t['kh'].astype(BF16)
        t['qk_kk'] = lax.dot_general(jnp.concatenate([t['qh'].astype(BF16), kb], axis=0), kb, nt,
                                     preferred_element_type=F32)
    for t in hs:
        decay = jnp.exp(jnp.where(t['incl'], t['gcol'] - t['grow'], -jnp.inf))
        t['qk'] = t['qk_kk'][0:c] * decay
        a = jnp.where(t['strict'], t['qk_kk'][c:2 * c] * t['beta'] * decay, 0.0)
        t['pw'] = a
        t['inv'] = eye - a
    for _ in range(5):
        for t in hs:
            t['pw'] = _bdot(t['pw'], t['pw'])
        for t in hs:
            t['inv'] = t['inv'] + _bdot(t['inv'], t['pw'])
    for t in hs:
        eg = jnp.exp(t['gcol'])
        rhs = jnp.concatenate([t['vv'] * t['beta'], t['kh'] * (t['beta'] * eg)], axis=1)
        t['uw'] = _bdot(t['inv'], rhs)
        t['qd'] = t['qh'] * eg
    for t in hs:
        t['s_old'] = s_ref[t['vh']]
        t['ws_qs'] = _bdot(jnp.concatenate([t['uw'][:, LANE:2 * LANE], t['qd']], axis=0), t['s_old'])
    for t in hs:
        t['v_new'] = t['uw'][:, 0:LANE] - t['ws_qs'][0:c]
        t['o_ref'][:, t['sl']] = t['ws_qs'][c:2 * c] + _bdot(t['qk'], t['v_new'])
    for t in hs:
        kt = t['kh'] * jnp.exp(t['g_last'] - t['gcol'])
        s_ref[t['vh']] = t['s_old'] * jnp.exp(t['g_last']) + _bdot(kt.T, t['v_new'])


def gdn_call(qkv, bg, gt, n_lat):
    t = qkv.shape[1]
    c = GDN_CHUNK
    n = t // c
    n_ctx = (t - n_lat) // c

    def fwd(s):
        return (s + n - n_ctx) % n

    def rev(s):
        return n - 1 - s

    def specs(order):
        return [pl.BlockSpec((1, c, GDN_W), lambda s: (0, order(s), 0)),
                pl.BlockSpec((1, c, GDN_W), lambda s: (1, order(s), 0)),
                pl.BlockSpec((1, c, GDN_W), lambda s: (2, order(s), 0)),
                pl.BlockSpec((c, LANE), lambda s: (order(s), 0)),
                pl.BlockSpec((1, 2 * GDN_HEADS, c), lambda s: (order(s), 0, 0))]

    return pl.pallas_call(
        _gdn_kernel,
        out_shape=(jax.ShapeDtypeStruct((t, GDN_W), F32), jax.ShapeDtypeStruct((t, GDN_W), F32)),
        grid=(n,),
        in_specs=specs(fwd) + specs(rev),
        out_specs=(pl.BlockSpec((c, GDN_W), lambda s: (fwd(s), 0)), pl.BlockSpec((c, GDN_W), lambda s: (rev(s), 0))),
        scratch_shapes=[pltpu.VMEM((2 * GDN_HEADS, GDN_DK, GDN_DV), F32)],
        compiler_params=_cparams(("arbitrary",)),
        name="gdn_scan",
    )(qkv, qkv, qkv, bg, gt, qkv, qkv, qkv, bg, gt)


def _rg_direction(x_ref, p_ref, n_ref, cw_ref, cb_ref, wa_ref, ba_ref, wx_ref, bx_ref, lam_ref, o_ref,
                  h_ref, ext_ref, tile, n_lat_tiles, d):
    t = ROW_TILE
    has_prev, has_next = _halo_flags(tile, n_lat_tiles)
    x = _conv4(x_ref, p_ref, n_ref, cw_ref[...], ext_ref, has_prev, has_next) + cb_ref[...]
    xb = x.astype(BF16)
    r_parts, i_parts = [], []
    for n in range(RG_BLOCKS):
        seg = xb[:, n * RG_BS:(n + 1) * RG_BS]
        r_parts.append(jnp.dot(seg, wa_ref[0, n].astype(BF16), preferred_element_type=F32))
        i_parts.append(jnp.dot(seg, wx_ref[0, n].astype(BF16), preferred_element_type=F32))
    r = _sigmoid(jnp.concatenate(r_parts, axis=1) + ba_ref[0])
    ig = _sigmoid(jnp.concatenate(i_parts, axis=1) + bx_ref[0])
    log_a = -RG_C * r * _softplus(-lam_ref[0])
    a = jnp.exp(log_a)
    u = jnp.sqrt(_one_minus_exp(2.0 * log_a)) * (ig * x)
    sub = lax.broadcasted_iota(jnp.int32, (t, 1), 0) & (SUBLANES - 1)
    sh = 1
    while sh < SUBLANES:
        if d == 0:
            a_s = pltpu.roll(a, sh, 0)
            u_s = pltpu.roll(u, sh, 0)
            ok = sub >= sh
        else:
            a_s = pltpu.roll(a, t - sh, 0)
            u_s = pltpu.roll(u, t - sh, 0)
            ok = sub < SUBLANES - sh
        u = jnp.where(ok, a * u_s + u, u)
        a = jnp.where(ok, a * a_s, a)
        sh *= 2
    carry = h_ref[d:d + 1, :]
    n_groups = t // SUBLANES
    for g in (range(n_groups) if d == 0 else range(n_groups - 1, -1, -1)):
        rows = slice(g * SUBLANES, (g + 1) * SUBLANES)
        h_g = u[rows, :] + a[rows, :] * carry
        o_ref[rows, :] = h_g
        carry = h_g[SUBLANES - 1:SUBLANES, :] if d == 0 else h_g[0:1, :]
    h_ref[d:d + 1, :] = carry


def _rg_kernel(xf_ref, pf_ref, nf_ref, xr_ref, pr_ref, nr_ref, cw_ref, cb_ref,
               waf_ref, baf_ref, wxf_ref, bxf_ref, lamf_ref, war_ref, bar_ref, wxr_ref, bxr_ref, lamr_ref,
               of_ref, or_ref, h_ref, ext_ref, *, n_tiles, n_lat_tiles):
    s = pl.program_id(0)

    @pl.when(s == 0)
    def _():
        h_ref[...] = jnp.zeros(h_ref.shape, F32)

    tile_f = (s + n_lat_tiles) % n_tiles
    tile_r = n_tiles - 1 - s
    _rg_direction(xf_ref, pf_ref, nf_ref, cw_ref, cb_ref, waf_ref, baf_ref, wxf_ref, bxf_ref, lamf_ref, of_ref,
                  h_ref, ext_ref, tile_f, n_lat_tiles, 0)
    _rg_direction(xr_ref, pr_ref, nr_ref, cw_ref, cb_ref, war_ref, bar_ref, wxr_ref, bxr_ref, lamr_ref, or_ref,
                  h_ref, ext_ref, tile_r, n_lat_tiles, 1)


def rg_call(z, conv_w, conv_b, w_a, b_a, w_x, b_x, lam, n_lat):
    t = z.shape[0]
    n_tiles = t // ROW_TILE
    n_lat_tiles = n_lat // ROW_TILE
    r8 = ROW_TILE // 8
    last8 = n_tiles * r8 - 1
    colb = Z_RX // RG_WIDTH

    def fwd(s):
        return (s + n_lat_tiles) % n_tiles

    def rev(s):
        return n_tiles - 1 - s

    def tile_specs(order):
        return [pl.BlockSpec((ROW_TILE, RG_WIDTH), lambda s: (order(s), colb)),
                pl.BlockSpec((8, RG_WIDTH), lambda s: (jnp.maximum(order(s) * r8 - 1, 0), colb)),
                pl.BlockSpec((8, RG_WIDTH), lambda s: (jnp.minimum((order(s) + 1) * r8, last8), colb))]

    def dir_specs(d):
        return [pl.BlockSpec((1, RG_BLOCKS, RG_BS, RG_BS), lambda s: (d, 0, 0, 0)),
                pl.BlockSpec((1, 1, RG_WIDTH), lambda s: (d, 0, 0)),
                pl.BlockSpec((1, RG_BLOCKS, RG_BS, RG_BS), lambda s: (d, 0, 0, 0)),
                pl.BlockSpec((1, 1, RG_WIDTH), lambda s: (d, 0, 0)),
                pl.BlockSpec((1, 1, RG_WIDTH), lambda s: (d, 0, 0))]

    b_a3 = b_a.reshape(2, 1, RG_WIDTH)
    b_x3 = b_x.reshape(2, 1, RG_WIDTH)
    lam3 = lam.reshape(2, 1, RG_WIDTH)
    return pl.pallas_call(
        functools.partial(_rg_kernel, n_tiles=n_tiles, n_lat_tiles=n_lat_tiles),
        out_shape=(jax.ShapeDtypeStruct((t, RG_WIDTH), F32), jax.ShapeDtypeStruct((t, RG_WIDTH), F32)),
        grid=(n_tiles,),
        in_specs=tile_specs(fwd) + tile_specs(rev)
        + [pl.BlockSpec((4, RG_WIDTH), lambda s: (0, 0)), pl.BlockSpec((1, RG_WIDTH), lambda s: (0, 0))]
        + dir_specs(0) + dir_specs(1),
        out_specs=(pl.BlockSpec((ROW_TILE, RG_WIDTH), lambda s: (fwd(s), 0)),
                   pl.BlockSpec((ROW_TILE, RG_WIDTH), lambda s: (rev(s), 0))),
        scratch_shapes=[pltpu.VMEM((8, RG_WIDTH), F32), pltpu.VMEM((ROW_TILE + 16, RG_WIDTH), F32)],
        compiler_params=_cparams(("arbitrary",)),
        name="rg_lru",
    )(z, z, z, z, z, z, conv_w, conv_b, w_a, b_a3, w_x, b_x3, lam3, w_a, b_a3, w_x, b_x3, lam3)


def _assemble_kernel(of_ref, or_ref, gz_ref, on_ref, hf_ref, hr_ref, ry_ref, og_ref, org_ref):
    o = of_ref[...] + or_ref[...]
    gate = _silu(gz_ref[...])
    for h in range(GDN_HEADS):
        sl = slice(h * LANE, (h + 1) * LANE)
        seg = o[:, sl]
        on = seg * lax.rsqrt(jnp.mean(seg * seg, axis=-1, keepdims=True) + EPS) * on_ref[...]
        og_ref[:, sl] = (on * gate[:, sl]).astype(BF16)
    org_ref[...] = ((hf_ref[...] + hr_ref[...]) * _gelu_tanh(ry_ref[...])).astype(BF16)


def assemble_call(of, orv, z, o_norm, hf, hr, m):
    row = pl.BlockSpec((ROW_TILE, 1024), lambda i: (i, 0))
    return pl.pallas_call(
        _assemble_kernel,
        out_shape=(jax.ShapeDtypeStruct((m, GDN_W), BF16), jax.ShapeDtypeStruct((m, RG_WIDTH), BF16)),
        grid=(m // ROW_TILE,),
        in_specs=[row, row, pl.BlockSpec((ROW_TILE, GDN_W), lambda i: (i, Z_GZ // GDN_W)),
                  pl.BlockSpec((1, LANE), lambda i: (0, 0)), row, row,
                  pl.BlockSpec((ROW_TILE, RG_WIDTH), lambda i: (i, Z_RY // RG_WIDTH))],
        out_specs=(row, row),
        compiler_params=_cparams(("arbitrary",)),
        name="mix_assemble",
    )(of, orv, z, o_norm, hf, hr, z)


def _post_mix_kernel(y_ref, x_ref, gpost_ref, gm_ref, gpre_ref, sc_ref, sh_ref, wr_ref,
                     x1_ref, h2_ref, aff_ref, *, ctx_tile):
    is_ctx = pl.program_id(0) >= ctx_tile

    def pick(ref):
        return jnp.where(is_ctx, ref[1:2, :], ref[0:1, :])

    x1 = x_ref[...] + pick(gm_ref) * _rms(y_ref[...], gpost_ref[...])
    x1_ref[...] = x1
    h2 = _rms(x1, gpre_ref[...]) * (1.0 + pick(sc_ref)) + pick(sh_ref)
    h2b = h2.astype(BF16)
    h2r = h2b.astype(F32)
    h2_ref[...] = _pack_bf16_pair(h2r[:, 0:D_MODEL // 2], h2r[:, D_MODEL // 2:])
    h_lo = (h2 - h2b.astype(F32)).astype(BF16)
    wr = wr_ref[...]
    w_hi = wr.astype(BF16)
    w_lo = (wr - w_hi.astype(F32)).astype(BF16)
    nt = (((1,), (1,)), ((), ()))
    logits = (lax.dot_general(w_hi, h2b, nt, preferred_element_type=F32)
              + lax.dot_general(w_hi, h_lo, nt, preferred_element_type=F32)
              + lax.dot_general(w_lo, h2b, nt, preferred_element_type=F32))
    e = jnp.exp(logits - jnp.max(logits, axis=0, keepdims=True))
    aff_ref[...] = e / jnp.sum(e, axis=0, keepdims=True)


def post_mix_call(y, x, gpost, gm, gpre, sc, sh, wr_t, n_lat):
    m = y.shape[0]
    row = pl.BlockSpec((ROW_TILE, D_MODEL), lambda i: (i, 0))
    vec1 = pl.BlockSpec((1, D_MODEL), lambda i: (0, 0))
    vec2 = pl.BlockSpec((2, D_MODEL), lambda i: (0, 0))
    return pl.pallas_call(
        functools.partial(_post_mix_kernel, ctx_tile=n_lat // ROW_TILE),
        out_shape=(jax.ShapeDtypeStruct((m, D_MODEL), F32), jax.ShapeDtypeStruct((m, D_MODEL // 2), jnp.uint32),
                   jax.ShapeDtypeStruct((N_EXPERTS, m), F32)),
        grid=(m // ROW_TILE,),
        in_specs=[row, row, vec1, vec2, vec1, vec2, vec2, pl.BlockSpec((N_EXPERTS, D_MODEL), lambda i: (0, 0))],
        out_specs=(row, pl.BlockSpec((ROW_TILE, D_MODEL // 2), lambda i: (i, 0)),
                   pl.BlockSpec((N_EXPERTS, ROW_TILE), lambda i: (0, i))),
        compiler_params=_cparams(("arbitrary",)),
        name="post_mix",
    )(y, x, gpost, gm, gpre, sc, sh, wr_t)


def _post_ffn_kernel(y_ref, x_ref, g_ref, gf_ref, o_ref, *, ctx_tile):
    is_ctx = pl.program_id(0) >= ctx_tile
    gf = jnp.where(is_ctx, gf_ref[1:2, :], gf_ref[0:1, :])
    o_ref[...] = x_ref[...] + gf * _rms(y_ref[...], g_ref[...])


def post_ffn_call(y, x, gain, gf, n_lat):
    m = y.shape[0]
    row = pl.BlockSpec((ROW_TILE, D_MODEL), lambda i: (i, 0))
    return pl.pallas_call(
        functools.partial(_post_ffn_kernel, ctx_tile=n_lat // ROW_TILE),
        out_shape=jax.ShapeDtypeStruct((m, D_MODEL), F32),
        grid=(m // ROW_TILE,),
        in_specs=[row, row, pl.BlockSpec((1, D_MODEL), lambda i: (0, 0)), pl.BlockSpec((2, D_MODEL), lambda i: (0, 0))],
        out_specs=row,
        compiler_params=_cparams(("arbitrary",)),
        name="post_ffn",
    )(y, x, gain, gf)


def _pack_bf16_pair(lo, hi):
    lo_bits = lax.bitcast_convert_type(lo, jnp.uint32) >> 16
    hi_bits = lax.bitcast_convert_type(hi, jnp.uint32) & jnp.uint32(0xFFFF0000)
    return lo_bits | hi_bits


def _unpack_bf16_pair(w):
    lo = lax.bitcast_convert_type(w << 16, F32).astype(BF16)
    hi = lax.bitcast_convert_type(w & jnp.uint32(0xFFFF0000), F32).astype(BF16)
    return lo, hi


def _moe_up_kernel(idx_ref, h_hbm, wg_ref, wu_ref, o_ref, xbuf, sem, *, cap, n_ff_tiles):
    e = pl.program_id(0)
    n_e = pl.num_programs(0)
    half = D_MODEL // 2

    f = pl.program_id(1)
    n_f = pl.num_programs(1)
    slot = e % 2
    part = cap // n_ff_tiles

    def row_copy(expert, s, dst_slot):
        row = idx_ref[expert, s]
        return pltpu.make_async_copy(h_hbm.at[pl.ds(row, 1), :], xbuf.at[dst_slot, pl.ds(s, 1), :], sem.at[dst_slot])

    def wait_rows(dst_slot):
        pltpu.make_async_copy(h_hbm.at[pl.ds(0, cap), :], xbuf.at[dst_slot], sem.at[dst_slot]).wait()

    @pl.when(f == 0)
    def _():
        @pl.when(e == 0)
        def _():
            def body(s, carry):
                row_copy(0, s, 0).start()
                return carry
            lax.fori_loop(0, cap, body, 0)

        wait_rows(slot)

    nxt = jnp.minimum(e + 1, n_e - 1)
    for j in range(part):
        row_copy(nxt, f * part + j, 1 - slot).start()

    x_lo, x_hi = _unpack_bf16_pair(xbuf[slot])

    def proj(w_ref):
        return (jnp.dot(x_lo, w_ref[0, 0, 0:half, :].astype(BF16), preferred_element_type=F32)
                + jnp.dot(x_hi, w_ref[0, 0, half:, :].astype(BF16), preferred_element_type=F32))

    o_ref[0] = (_silu(proj(wg_ref)) * proj(wu_ref)).astype(BF16)

    @pl.when(jnp.logical_and(e == n_e - 1, f == n_f - 1))
    def _():
        wait_rows(1 - slot)


def moe_up_call(idx, h_packed, w_gate, w_up, l, tf=256):
    e, cap = idx.shape
    d = D_MODEL
    return pl.pallas_call(
        functools.partial(_moe_up_kernel, cap=cap, n_ff_tiles=EXPERT_FF // tf),
        out_shape=jax.ShapeDtypeStruct((e, cap, EXPERT_FF), BF16),
        grid_spec=pltpu.PrefetchScalarGridSpec(
            num_scalar_prefetch=1,
            grid=(e, EXPERT_FF // tf),
            in_specs=[pl.BlockSpec(memory_space=pl.ANY),
                      pl.BlockSpec((1, 1, d, tf), lambda i, f, idx_ref: (l, i, 0, f)),
                      pl.BlockSpec((1, 1, d, tf), lambda i, f, idx_ref: (l, i, 0, f))],
            out_specs=pl.BlockSpec((1, cap, tf), lambda i, f, idx_ref: (i, 0, f)),
            scratch_shapes=[pltpu.VMEM((2, cap, d // 2), jnp.uint32), pltpu.SemaphoreType.DMA((2,))]),
        compiler_params=_cparams(("arbitrary", "arbitrary")),
        name="moe_up",
    )(idx, h_packed, w_gate, w_up)


def _moe_scatter_kernel(idx_ref, y_ref, acc_in, acc_hbm, buf, sem_in, sem_out, *, rows):
    del acc_in
    e = pl.program_id(0)
    base = pl.program_id(1) * rows

    def gather(s, carry):
        row = idx_ref[e, base + s]
        pltpu.make_async_copy(acc_hbm.at[pl.ds(row, 1), :], buf.at[pl.ds(s, 1), :], sem_in).start()
        return carry

    def scatter(s, carry):
        row = idx_ref[e, base + s]
        pltpu.make_async_copy(buf.at[pl.ds(s, 1), :], acc_hbm.at[pl.ds(row, 1), :], sem_out).start()
        return carry

    lax.fori_loop(0, rows, gather, 0, unroll=8)
    pltpu.make_async_copy(acc_hbm.at[pl.ds(0, rows), :], buf, sem_in).wait()
    buf[...] = buf[...] + y_ref[0]
    lax.fori_loop(0, rows, scatter, 0, unroll=8)
    pltpu.make_async_copy(buf, acc_hbm.at[pl.ds(0, rows), :], sem_out).wait()


def moe_scatter_call(idx, y_e, m):
    e, cap, d = y_e.shape
    rows = cap // 2
    return pl.pallas_call(
        functools.partial(_moe_scatter_kernel, rows=rows),
        out_shape=jax.ShapeDtypeStruct((m, d), F32),
        grid_spec=pltpu.PrefetchScalarGridSpec(
            num_scalar_prefetch=1,
            grid=(e, 2),
            in_specs=[pl.BlockSpec((1, rows, d), lambda i, hf, idx_ref: (i, hf, 0)),
                      pl.BlockSpec(memory_space=pl.ANY)],
            out_specs=pl.BlockSpec(memory_space=pl.ANY),
            scratch_shapes=[pltpu.VMEM((rows, d), F32), pltpu.SemaphoreType.DMA(()), pltpu.SemaphoreType.DMA(())]),
        input_output_aliases={2: 0},
        compiler_params=_cparams(("arbitrary", "arbitrary")),
        name="moe_scatter",
    )(idx, y_e, jnp.zeros((m, d), F32))


def _moe_down_kernel(h_ref, wd_ref, g_ref, o_ref):
    y = jnp.dot(h_ref[0], wd_ref[0, 0].astype(BF16), preferred_element_type=F32)
    o_ref[0] = y * g_ref[0]


def moe_down_call(hid, w_down, l, gates, tn=1024):
    e, c, f = hid.shape
    d = w_down.shape[3]
    return pl.pallas_call(
        _moe_down_kernel,
        out_shape=jax.ShapeDtypeStruct((e, c, d), F32),
        grid=(e, d // tn),
        in_specs=[pl.BlockSpec((1, c, f), lambda i, n: (i, 0, 0)),
                  pl.BlockSpec((1, 1, f, tn), lambda i, n: (l, i, 0, n)),
                  pl.BlockSpec((1, c, 1), lambda i, n: (i, 0, 0))],
        out_specs=pl.BlockSpec((1, c, tn), lambda i, n: (i, 0, n)),
        compiler_params=_cparams(("arbitrary", "arbitrary")),
        name="moe_down",
    )(hid, w_down, gates)


def _rope_tables(n_lat, n_ctx):
    rows = n_lat // GRID_W
    row = jnp.broadcast_to(jnp.arange(rows, dtype=F32)[:, None], (rows, GRID_W)).reshape(-1)
    col = jnp.broadcast_to(jnp.arange(GRID_W, dtype=F32)[None, :], (rows, GRID_W)).reshape(-1)
    n_freq = MLA_ROPE // 4
    inv_freq = ROPE_THETA ** (-jnp.arange(n_freq, dtype=F32) / n_freq)
    ang = jnp.concatenate([row[:, None] * inv_freq, col[:, None] * inv_freq], axis=-1)
    cos = jnp.cos(ang)
    sin = jnp.sin(ang)
    cos64 = jnp.concatenate([cos, cos], axis=-1)
    sin64 = jnp.concatenate([sin, sin], axis=-1)
    one_l = jnp.ones((n_lat, 64), F32)
    zero_l = jnp.zeros((n_lat, 64), F32)
    one_c = jnp.ones((n_ctx, 64), F32)
    zero_c = jnp.zeros((n_ctx, 64), F32)
    cq = jnp.concatenate([jnp.concatenate([cos64, one_l], 1), jnp.concatenate([one_c, one_c], 1)], 0)
    sq = jnp.concatenate([jnp.concatenate([sin64, zero_l], 1), jnp.concatenate([zero_c, zero_c], 1)], 0)
    ck = jnp.concatenate([jnp.concatenate([cos64, zero_l], 1), jnp.concatenate([zero_c, one_c], 1)], 0)
    sk = jnp.concatenate([jnp.concatenate([sin64, zero_l], 1), jnp.concatenate([zero_c, zero_c], 1)], 0)
    return cq, sq, ck, sk


def _rot_half_cols(w):
    half = MLA_ROPE // 2
    return jnp.concatenate([-w[..., half:], w[..., :half]], axis=-1)


def _relayout_w_in(w_in):
    q_a = w_in[:, 0:1024]
    kv_a = w_in[:, 1024:1536]
    k_pe = w_in[:, 1536:1600]
    g_qkv = w_in[:, 1600:4672]
    g_z = w_in[:, 4672:5696]
    g_ba = w_in[:, 5696:5728]
    r_x = w_in[:, 5728:6752]
    r_y = w_in[:, 6752:7776]
    zeros = functools.partial(jnp.zeros, dtype=w_in.dtype)
    d = w_in.shape[0]
    return jnp.concatenate([q_a, g_qkv, g_z, r_x, r_y, kv_a, k_pe, k_pe, _rot_half_cols(k_pe), zeros((d, 64)),
                            g_ba, zeros((d, LANE - 32)), zeros((d, LANE))], axis=1)


def _relayout_w_qb(w_qb):
    w = w_qb.reshape(MLA_Q_RANK, MLA_HEADS, MLA_NOPE + MLA_ROPE)
    nope = w[:, :, :MLA_NOPE]
    pe = w[:, :, MLA_NOPE:]
    zeros = jnp.zeros_like(pe)
    out = jnp.concatenate([nope, pe, pe, _rot_half_cols(pe), zeros], axis=-1)
    return jnp.transpose(out, (1, 0, 2))


BIG_WEIGHTS = ('mod_w', 'mla_w_kvb', 'w_out', 'w_gate', 'w_up', 'w_down')


def _layer(x_all, c_rows, p, big, l, n_lat, n_ctx, update_ctx):
    t = n_lat + n_ctx
    mod = mod_call(c_rows, big['mod_w'], p['mod_b'][None, :], l)[0:2]
    sh_m, sc_m, g_m, sh_f, sc_f, g_f = [mod[:, i * D_MODEL:(i + 1) * D_MODEL] for i in range(6)]

    h = prenorm_call(x_all, p['norm_mix_pre'][None, :], sc_m, sh_m, n_lat)
    z = matmul_call(h, _relayout_w_in(p['w_in']), name="w_in")

    cq, sq, ck, sk = _rope_tables(n_lat, n_ctx)
    q = qproj_call(z, p['mla_q_norm'][None, :], _relayout_w_qb(p['mla_w_qb']), cq, sq)
    tk = next(c for c in (768, 256) if t % c == 0)
    k, v = kvproj_call(z, p['mla_kv_norm'][None, :], big['mla_w_kvb'], l, ck, sk, tk)
    o_mla = attn_call(q, k, v, n_lat, 0, t, 0, tq=512, tk=tk)
    if update_ctx:
        o_mla_c = attn_call(q, k, v, n_ctx, n_lat, n_ctx, n_lat, tq=n_ctx, tk=n_ctx)
        o_mla = jnp.concatenate([o_mla, o_mla_c], axis=0)

    pad = jnp.zeros((2 * GDN_HEADS,), F32)
    alog_row = jnp.concatenate([pad, p['gdn_a_log'].reshape(-1), jnp.zeros((LANE - 4 * GDN_HEADS,), F32)])[None, :]
    dtb_row = jnp.concatenate([pad, p['gdn_dt_bias'].reshape(-1), jnp.zeros((LANE - 4 * GDN_HEADS,), F32)])[None, :]
    qkv, bg = gdn_prep_call(z, p['gdn_conv_w'], alog_row, dtb_row, n_lat)
    gt = jnp.swapaxes(bg[:, 2 * GDN_HEADS:4 * GDN_HEADS].reshape(t // GDN_CHUNK, GDN_CHUNK, 2 * GDN_HEADS), 1, 2)
    o_f, o_r = gdn_call(qkv, bg, gt, n_lat)

    h_f, h_r = rg_call(z, p['rg_conv_w'], p['rg_conv_b'][None, :], p['rg_w_a'], p['rg_b_a'], p['rg_w_x'],
                       p['rg_b_x'], p['rg_lambda'], n_lat)

    m = t if update_ctx else n_lat
    o_gdn, o_rg = assemble_call(o_f, o_r, z, p['gdn_o_norm'][None, :], h_f, h_r, m)
    y = matmul3_call(o_mla, o_gdn, o_rg, big['w_out'], l, m)
    x1, h2, aff_t = post_mix_call(y, x_all, p['norm_mix_post'][None, :], g_m, p['norm_ffn_pre'][None, :],
                                  sc_f, sh_f, p['w_router'].T, n_lat)

    cap = CAPACITY_FACTOR * n_lat // N_EXPERTS
    gates, idx = lax.top_k(aff_t[:, :n_lat], cap)
    if update_ctx:
        cap_c = CAPACITY_FACTOR * n_ctx // N_EXPERTS
        gates_c, idx_c = lax.top_k(aff_t[:, n_lat:], cap_c)
        gates = jnp.concatenate([gates, gates_c], axis=1)
        idx = jnp.concatenate([idx, idx_c + n_lat], axis=1)
    hid = moe_up_call(idx, h2, big['w_gate'], big['w_up'], l)
    y_e = moe_down_call(hid, big['w_down'], l, gates[..., None])
    y_moe = moe_scatter_call(idx, y_e, m)
    return post_ffn_call(y_moe, x1, p['norm_ffn_post'][None, :], g_f, n_lat)


def kernel(x, c, ctx, c_ctx, mod_w, mod_b, norm_mix_pre, norm_mix_post, norm_ffn_pre, norm_ffn_post,
           w_in, mla_q_norm, mla_kv_norm, mla_w_qb, mla_w_kvb, gdn_conv_w, gdn_a_log, gdn_dt_bias,
           gdn_o_norm, rg_conv_w, rg_conv_b, rg_w_a, rg_b_a, rg_w_x, rg_b_x, rg_lambda, w_out,
           w_router, w_gate, w_up, w_down):
    assert x.shape[0] == 1 and ctx.shape[0] == 1
    n_lat, n_ctx = x.shape[1], ctx.shape[1]
    assert n_ctx == ROW_TILE and n_lat % 512 == 0
    depth = mod_w.shape[0]
    stacked = dict(mod_w=mod_w, mod_b=mod_b, norm_mix_pre=norm_mix_pre, norm_mix_post=norm_mix_post,
                   norm_ffn_pre=norm_ffn_pre, norm_ffn_post=norm_ffn_post, w_in=w_in, mla_q_norm=mla_q_norm,
                   mla_kv_norm=mla_kv_norm, mla_w_qb=mla_w_qb, mla_w_kvb=mla_w_kvb, gdn_conv_w=gdn_conv_w,
                   gdn_a_log=gdn_a_log, gdn_dt_bias=gdn_dt_bias, gdn_o_norm=gdn_o_norm, rg_conv_w=rg_conv_w,
                   rg_conv_b=rg_conv_b, rg_w_a=rg_w_a, rg_b_a=rg_b_a, rg_w_x=rg_w_x, rg_b_x=rg_b_x,
                   rg_lambda=rg_lambda, w_out=w_out, w_router=w_router, w_gate=w_gate, w_up=w_up, w_down=w_down)
    c_rows = jnp.concatenate([c, c_ctx[None, :], jnp.zeros((6, D_MODEL), F32)], axis=0)
    x_all = jnp.concatenate([x[0], ctx[0]], axis=0)
    big = {k: stacked[k] for k in BIG_WEIGHTS}
    for l in range(depth):
        p = {k: v[l] for k, v in stacked.items() if k not in BIG_WEIGHTS}
        x_all = _layer(x_all, c_rows, p, big, l, n_lat, n_ctx, l < depth - 1)
    return x_all[None]
```

```python
import functools
import math

import jax
import jax.numpy as jnp
from jax import lax
from jax.experimental import pallas as pl
from jax.experimental.pallas import tpu as pltpu

F32 = jnp.float32
BF16 = jnp.bfloat16

D_MODEL = 4096
EPS = 1e-6
GRID_W = 64
ROPE_THETA = 10000.0

MLA_HEADS = 16
MLA_NOPE = 128
MLA_ROPE = 64
MLA_V = 128
MLA_Q_RANK = 1024
MLA_KV_RANK = 512
VT_ROWS = MLA_V + 16

GDN_HEADS = 8
GDN_DK = 128
GDN_DV = 128
GDN_CHUNK = 64
GDN_W = GDN_HEADS * GDN_DK

RG_WIDTH = 1024
RG_BLOCKS = 8
RG_BS = RG_WIDTH // RG_BLOCKS
RG_C = 8.0

N_EXPERTS = 16
EXPERT_FF = 1024
CAPACITY_FACTOR = 2

LANE = 128
SUBLANES = 8
ROW_TILE = 256
VMEM_LIMIT = 56 * 1024 * 1024

Z_QA, Z_GQ, Z_GK, Z_GV, Z_GZ, Z_RX, Z_RY = 0, 1024, 2048, 3072, 4096, 5120, 6144
Z_KVA = 7168
Z_PEA, Z_PEB, Z_BA = 7680, 7808, 7936
Z_COLS = 8192


def _cparams(sem, vmem=VMEM_LIMIT):
    return pltpu.CompilerParams(dimension_semantics=sem, vmem_limit_bytes=vmem)


def _sigmoid(x):
    return 1.0 / (1.0 + jnp.exp(-x))


def _silu(x):
    return x * _sigmoid(x)


def _softplus(x):
    return jnp.maximum(x, 0.0) + jnp.log(1.0 + jnp.exp(-jnp.abs(x)))


def _one_minus_exp(y):
    poly = 1.0 / math.factorial(8)
    for j in range(6, -1, -1):
        poly = poly * y + 1.0 / math.factorial(j + 1)
    return jnp.where(y > -0.125, -(y * poly), 1.0 - jnp.exp(y))


def _gelu_tanh(x):
    return 0.5 * x * (1.0 + jnp.tanh(math.sqrt(2.0 / math.pi) * (x + 0.044715 * (x * x * x))))


def _rms(x, gain):
    return x * lax.rsqrt(jnp.mean(x * x, axis=-1, keepdims=True) + EPS) * gain


def _mod_kernel(c_ref, w_ref, b_ref, o_ref):
    s = _silu(c_ref[...])
    o_ref[...] = jnp.dot(s.astype(BF16), w_ref[0].astype(BF16), preferred_element_type=F32) + b_ref[...]


def mod_call(cc, w, b, l):
    tn = 1024
    n = w.shape[2]
    return pl.pallas_call(
        _mod_kernel,
        out_shape=jax.ShapeDtypeStruct((8, n), F32),
        grid=(n // tn,),
        in_specs=[pl.BlockSpec((8, D_MODEL), lambda j: (0, 0)),
                  pl.BlockSpec((1, D_MODEL, tn), lambda j: (l, 0, j)),
                  pl.BlockSpec((1, tn), lambda j: (0, j))],
        out_specs=pl.BlockSpec((8, tn), lambda j: (0, j)),
        compiler_params=_cparams(("arbitrary",)),
        name="mod",
    )(cc, w, b)


def _prenorm_kernel(x_ref, g_ref, sc_ref, sh_ref, o_ref, *, ctx_tile):
    is_ctx = pl.program_id(0) >= ctx_tile
    sc = jnp.where(is_ctx, sc_ref[1:2, :], sc_ref[0:1, :])
    sh = jnp.where(is_ctx, sh_ref[1:2, :], sh_ref[0:1, :])
    y = _rms(x_ref[...], g_ref[...])
    o_ref[...] = (y * (1.0 + sc) + sh).astype(BF16)


def prenorm_call(x, gain, sc, sh, n_lat):
    t = x.shape[0]
    row = pl.BlockSpec((ROW_TILE, D_MODEL), lambda i: (i, 0))
    return pl.pallas_call(
        functools.partial(_prenorm_kernel, ctx_tile=n_lat // ROW_TILE),
        out_shape=jax.ShapeDtypeStruct((t, D_MODEL), BF16),
        grid=(t // ROW_TILE,),
        in_specs=[row, pl.BlockSpec((1, D_MODEL), lambda i: (0, 0)),
                  pl.BlockSpec((2, D_MODEL), lambda i: (0, 0)), pl.BlockSpec((2, D_MODEL), lambda i: (0, 0))],
        out_specs=row,
        compiler_params=_cparams(("arbitrary",)),
        name="prenorm",
    )(x, gain, sc, sh)


def _matmul_kernel(a_ref, w_ref, o_ref):
    o_ref[...] = jnp.dot(a_ref[...], w_ref[...].astype(BF16), preferred_element_type=F32)


def _pick_tm(m):
    for tm in (1056, 1024, 768, 512, 256):
        if m % tm == 0:
            return tm
    raise ValueError(f"unsupported row count {m}")


def matmul_call(a, w, tn=512, name="matmul"):
    m, k = a.shape
    n = w.shape[1]
    tm = _pick_tm(m)
    return pl.pallas_call(
        _matmul_kernel,
        out_shape=jax.ShapeDtypeStruct((m, n), F32),
        grid=(m // tm, n // tn),
        in_specs=[pl.BlockSpec((tm, k), lambda i, j: (i, 0)), pl.BlockSpec((k, tn), lambda i, j: (0, j))],
        out_specs=pl.BlockSpec((tm, tn), lambda i, j: (i, j)),
        compiler_params=_cparams(("arbitrary", "arbitrary")),
        name=name,
    )(a, w)


def _matmul3_kernel(a1_ref, a2_ref, a3_ref, w_ref, o_ref):
    k1 = a1_ref.shape[1]
    k2 = a2_ref.shape[1]
    acc = jnp.dot(a1_ref[...], w_ref[0, 0:k1, :].astype(BF16), preferred_element_type=F32)
    acc += jnp.dot(a2_ref[...], w_ref[0, k1:k1 + k2, :].astype(BF16), preferred_element_type=F32)
    acc += jnp.dot(a3_ref[...], w_ref[0, k1 + k2:, :].astype(BF16), preferred_element_type=F32)
    o_ref[...] = acc


def matmul3_call(a1, a2, a3, w, l, m, tn=512):
    k = w.shape[1]
    n = w.shape[2]
    tm = _pick_tm(m)
    return pl.pallas_call(
        _matmul3_kernel,
        out_shape=jax.ShapeDtypeStruct((m, n), F32),
        grid=(m // tm, n // tn),
        in_specs=[pl.BlockSpec((tm, a1.shape[1]), lambda i, j: (i, 0)),
                  pl.BlockSpec((tm, a2.shape[1]), lambda i, j: (i, 0)),
                  pl.BlockSpec((tm, a3.shape[1]), lambda i, j: (i, 0)),
                  pl.BlockSpec((1, k, tn), lambda i, j: (l, 0, j))],
        out_specs=pl.BlockSpec((tm, tn), lambda i, j: (i, j)),
        compiler_params=_cparams(("arbitrary", "arbitrary")),
        name="w_out",
    )(a1, a2, a3, w)


Q_HEAD_GROUP = 4


def _qproj_kernel(z_ref, g_ref, w_ref, c_ref, s_ref, o_ref, zn_ref, *, scale):
    @pl.when(pl.program_id(1) == 0)
    def _():
        zn_ref[...] = _rms(z_ref[...], g_ref[...]).astype(BF16)

    for g in range(Q_HEAD_GROUP):
        r = jnp.dot(zn_ref[...], w_ref[g].astype(BF16), preferred_element_type=F32)
        q0 = r[:, 0:LANE]
        q1 = r[:, LANE:2 * LANE] * c_ref[...] + r[:, 2 * LANE:3 * LANE] * s_ref[...]
        o_ref[g, :, 0:LANE] = (q0 * scale).astype(BF16)
        o_ref[g, :, LANE:2 * LANE] = (q1 * scale).astype(BF16)


def qproj_call(z, q_norm, wq, cq, sq):
    t = z.shape[0]
    tm = _pick_tm(t)
    scale = (MLA_NOPE + MLA_ROPE) ** -0.5 * math.log2(math.e)
    return pl.pallas_call(
        functools.partial(_qproj_kernel, scale=scale),
        out_shape=jax.ShapeDtypeStruct((MLA_HEADS, t, 2 * LANE), BF16),
        grid=(t // tm, MLA_HEADS // Q_HEAD_GROUP),
        in_specs=[pl.BlockSpec((tm, MLA_Q_RANK), lambda i, h: (i, Z_QA // MLA_Q_RANK)),
                  pl.BlockSpec((1, MLA_Q_RANK), lambda i, h: (0, 0)),
                  pl.BlockSpec((Q_HEAD_GROUP, MLA_Q_RANK, 3 * LANE), lambda i, h: (h, 0, 0)),
                  pl.BlockSpec((tm, LANE), lambda i, h: (i, 0)),
                  pl.BlockSpec((tm, LANE), lambda i, h: (i, 0))],
        out_specs=pl.BlockSpec((Q_HEAD_GROUP, tm, 2 * LANE), lambda i, h: (h, i, 0)),
        scratch_shapes=[pltpu.VMEM((tm, MLA_Q_RANK), BF16)],
        compiler_params=_cparams(("arbitrary", "arbitrary")),
        name="q_proj",
    )(z, q_norm, wq, cq, sq)


KV_HEAD_GROUP = 4


def _kvproj_kernel(z_ref, g_ref, w_ref, pa_ref, pb_ref, c_ref, s_ref, k_ref, v_ref, zn_ref):
    @pl.when(pl.program_id(1) == 0)
    def _():
        zn_ref[...] = _rms(z_ref[...], g_ref[...]).astype(BF16)

    r = jnp.dot(zn_ref[...], w_ref[0].astype(BF16), preferred_element_type=F32)
    k1 = (pa_ref[...] * c_ref[...] + pb_ref[...] * s_ref[...]).astype(BF16)
    for g in range(KV_HEAD_GROUP):
        c0 = g * 2 * LANE
        k_ref[g, :, 0:LANE] = r[:, c0:c0 + LANE].astype(BF16)
        k_ref[g, :, LANE:2 * LANE] = k1
        v_ref[g, 0, 0:MLA_V, :] = r[:, c0 + LANE:c0 + 2 * LANE].T.astype(BF16)
        v_ref[g, 0, MLA_V:VT_ROWS, :] = jnp.ones((VT_ROWS - MLA_V, r.shape[0]), BF16)


def kvproj_call(z, kv_norm, wkv, l, ck, sk, tm):
    t = z.shape[0]
    return pl.pallas_call(
        _kvproj_kernel,
        out_shape=(jax.ShapeDtypeStruct((MLA_HEADS, t, 2 * LANE), BF16),
                   jax.ShapeDtypeStruct((MLA_HEADS, t // tm, VT_ROWS, tm), BF16)),
        grid=(t // tm, MLA_HEADS // KV_HEAD_GROUP),
        in_specs=[pl.BlockSpec((tm, MLA_KV_RANK), lambda i, h: (i, Z_KVA // MLA_KV_RANK)),
                  pl.BlockSpec((1, MLA_KV_RANK), lambda i, h: (0, 0)),
                  pl.BlockSpec((1, MLA_KV_RANK, KV_HEAD_GROUP * 2 * LANE), lambda i, h: (l, 0, h)),
                  pl.BlockSpec((tm, LANE), lambda i, h: (i, Z_PEA // LANE)),
                  pl.BlockSpec((tm, LANE), lambda i, h: (i, Z_PEB // LANE)),
                  pl.BlockSpec((tm, LANE), lambda i, h: (i, 0)),
                  pl.BlockSpec((tm, LANE), lambda i, h: (i, 0))],
        out_specs=(pl.BlockSpec((KV_HEAD_GROUP, tm, 2 * LANE), lambda i, h: (h, i, 0)),
                   pl.BlockSpec((KV_HEAD_GROUP, 1, VT_ROWS, tm), lambda i, h: (h, i, 0, 0))),
        scratch_shapes=[pltpu.VMEM((tm, MLA_KV_RANK), BF16)],
        compiler_params=_cparams(("arbitrary", "arbitrary")),
        name="kv_proj",
    )(z, kv_norm, wkv, z, z, ck, sk)


ATTN_LOOKAHEAD = 2
ATTN_TQ = 1024


def _attn_kernel(q_ref, k_ref, vt_ref, o_ref, m_ref, acc_ref, *bufs, tk, n_chunks):
    q = q_ref[0]
    nt = (((1,), (1,)), ((), ()))

    def scores(c):
        return lax.dot_general(k_ref[0, c * tk:(c + 1) * tk, :], q, nt, preferred_element_type=F32)

    def consume(c, s_ref):
        m_old = m_ref[...]
        m_new = jnp.maximum(m_old, jnp.max(s_ref[...], axis=0, keepdims=True))
        alpha = jnp.exp2(m_old - m_new)
        p = jnp.exp2(s_ref[...] - m_new)
        pv = jnp.dot(vt_ref[0, c], p.astype(BF16), preferred_element_type=F32)
        acc_ref[...] = alpha * acc_ref[...] + pv
        m_ref[...] = m_new

    m_ref[...] = jnp.full(m_ref.shape, -jnp.inf, F32)
    acc_ref[...] = jnp.zeros(acc_ref.shape, F32)
    for c in range(min(ATTN_LOOKAHEAD, n_chunks)):
        bufs[c][...] = scores(c)
    for c in range(n_chunks):
        if c + ATTN_LOOKAHEAD < n_chunks:
            bufs[(c + ATTN_LOOKAHEAD) % len(bufs)][...] = scores(c + ATTN_LOOKAHEAD)
        consume(c, bufs[c % len(bufs)])
    o_ref[...] = (acc_ref[0:MLA_V, :] / acc_ref[MLA_V:MLA_V + 1, :]).T.astype(BF16)


def attn_call(q, k, vt, n_q, q_row0, n_k, k_row0, tq, tk):
    ck = vt.shape[3]
    assert n_q % tq == 0 and q_row0 % tq == 0 and n_k % tk == 0 and k_row0 % n_k == 0
    n_chunks = n_k // tk
    if n_chunks > 1:
        assert tk == ck and k_row0 == 0
        vt_spec = pl.BlockSpec((1, n_chunks, VT_ROWS, ck), lambda h, j: (h, 0, 0, 0))
    else:
        assert (k_row0 % ck) % tk == 0
        vt_spec = pl.BlockSpec((1, 1, VT_ROWS, tk), lambda h, j: (h, k_row0 // ck, 0, (k_row0 % ck) // tk))
    qb, kb = q_row0 // tq, k_row0 // n_k
    return pl.pallas_call(
        functools.partial(_attn_kernel, tk=tk, n_chunks=n_chunks),
        out_shape=jax.ShapeDtypeStruct((n_q, MLA_HEADS * MLA_V), BF16),
        grid=(MLA_HEADS, n_q // tq),
        in_specs=[pl.BlockSpec((1, tq, 2 * LANE), lambda h, j: (h, j + qb, 0)),
                  pl.BlockSpec((1, n_k, 2 * LANE), lambda h, j: (h, kb, 0)),
                  vt_spec],
        out_specs=pl.BlockSpec((tq, MLA_V), lambda h, j: (j, h)),
        scratch_shapes=[pltpu.VMEM((1, tq), F32), pltpu.VMEM((VT_ROWS, tq), F32)]
        + [pltpu.VMEM((tk, tq), F32)] * (ATTN_LOOKAHEAD + 1),
        compiler_params=_cparams(("arbitrary", "arbitrary")),
        name="mla_attn",
    )(q, k, vt)


def _conv4(x_ref, p_ref, n_ref, w, ext_ref, has_prev, has_next):
    t = x_ref.shape[0]
    ext_ref[0:8, :] = jnp.where(has_prev, p_ref[...], 0.0)
    ext_ref[8:8 + t, :] = x_ref[...]
    ext_ref[8 + t:16 + t, :] = jnp.where(has_next, n_ref[...], 0.0)
    y = ext_ref[6:6 + t, :] * w[0:1, :]
    y += ext_ref[7:7 + t, :] * w[1:2, :]
    y += ext_ref[8:8 + t, :] * w[2:3, :]
    y += ext_ref[9:9 + t, :] * w[3:4, :]
    return y


def _halo_flags(i, n_lat_tiles):
    has_prev = jnp.logical_and(i > 0, i < n_lat_tiles)
    has_next = i < n_lat_tiles - 1
    return has_prev, has_next


def _halo_specs(width, col_of, n_tiles):
    r8 = ROW_TILE // 8
    last8 = n_tiles * r8 - 1
    main = pl.BlockSpec((ROW_TILE, width), lambda i, *a: (i, col_of(*a)))
    prev = pl.BlockSpec((8, width), lambda i, *a: (jnp.maximum(i * r8 - 1, 0), col_of(*a)))
    nxt = pl.BlockSpec((8, width), lambda i, *a: (jnp.minimum((i + 1) * r8, last8), col_of(*a)))
    return main, prev, nxt


def _gdn_prep_kernel(x_ref, p_ref, n_ref, w_ref, ba_ref, alog_ref, dtb_ref, o_ref, bg_ref, ext_ref, *, n_lat_tiles):
    i = pl.program_id(0)
    j = pl.program_id(1)
    has_prev, has_next = _halo_flags(i, n_lat_tiles)
    y = _silu(_conv4(x_ref, p_ref, n_ref, w_ref[...], ext_ref, has_prev, has_next))
    qscale = jnp.where(j == 0, GDN_DK ** -0.5, 1.0)
    is_v = j == 2
    for h in range(GDN_HEADS):
        seg = y[:, h * LANE:(h + 1) * LANE]
        nrm = seg * lax.rsqrt(jnp.sum(seg * seg, axis=-1, keepdims=True) + EPS) * qscale
        o_ref[0, :, h * LANE:(h + 1) * LANE] = jnp.where(is_v, seg, nrm)

    @pl.when(j == 0)
    def _():
        ba = ba_ref[...]
        lane = lax.broadcasted_iota(jnp.int32, ba.shape, 1)
        beta = _sigmoid(ba)
        g = -jnp.exp(alog_ref[...]) * _softplus(ba + dtb_ref[...])
        bg_ref[...] = jnp.where(lane < 2 * GDN_HEADS, beta, g)


def gdn_prep_call(z, conv_w, alog_row, dtb_row, n_lat):
    t = z.shape[0]
    n_tiles = t // ROW_TILE
    main, prev, nxt = _halo_specs(GDN_W, lambda j: Z_GQ // GDN_W + j, n_tiles)
    row128 = pl.BlockSpec((1, LANE), lambda i, j: (0, 0))
    return pl.pallas_call(
        functools.partial(_gdn_prep_kernel, n_lat_tiles=n_lat // ROW_TILE),
        out_shape=(jax.ShapeDtypeStruct((3, t, GDN_W), F32), jax.ShapeDtypeStruct((t, LANE), F32)),
        grid=(n_tiles, 3),
        in_specs=[main, prev, nxt,
                  pl.BlockSpec((4, GDN_W), lambda i, j: (0, j)),
                  pl.BlockSpec((ROW_TILE, LANE), lambda i, j: (i, Z_BA // LANE)),
                  row128, row128],
        out_specs=(pl.BlockSpec((1, ROW_TILE, GDN_W), lambda i, j: (j, i, 0)),
                   pl.BlockSpec((ROW_TILE, LANE), lambda i, j: (i, 0))),
        scratch_shapes=[pltpu.VMEM((ROW_TILE + 16, GDN_W), F32)],
        compiler_params=_cparams(("arbitrary", "arbitrary")),
        name="gdn_prep",
    )(z, z, z, conv_w, z, alog_row, dtb_row)


def _split3(x):
    hi = x.astype(BF16)
    r1 = x - hi.astype(F32)
    mid = r1.astype(BF16)
    lo = (r1 - mid.astype(F32)).astype(BF16)
    return hi, mid, lo


def _bdot(a, b):
    return jnp.dot(a.astype(BF16), b.astype(BF16), preferred_element_type=F32)


def _gdn_gates(q_ref, k_ref, v_ref, bg_ref, gt_ref, o_ref, d):
    c = GDN_CHUNK
    row = lax.broadcasted_iota(jnp.int32, (c, c), 0)
    col = lax.broadcasted_iota(jnp.int32, (c, c), 1)
    if d == 0:
        incl = row >= col
        strict = row > col
    else:
        incl = row <= col
        strict = row < col
    incl_t = (row <= col) if d == 0 else (row >= col)
    tri = jnp.where(incl, 1.0, 0.0).astype(BF16)
    tri_t = jnp.where(incl_t, 1.0, 0.0).astype(BF16)
    bg = bg_ref[...]
    g_hi, g_mid, g_lo = _split3(bg)
    gc_all = (jnp.dot(tri, g_hi, preferred_element_type=F32) + jnp.dot(tri, g_mid, preferred_element_type=F32)
              + jnp.dot(tri, g_lo, preferred_element_type=F32))
    t_hi, t_mid, t_lo = _split3(gt_ref[0])
    gr_all = (jnp.dot(t_hi, tri_t, preferred_element_type=F32) + jnp.dot(t_mid, tri_t, preferred_element_type=F32)
              + jnp.dot(t_lo, tri_t, preferred_element_type=F32))
    last = c - 1 if d == 0 else 0
    heads = []
    for h in range(GDN_HEADS):
        vh = d * GDN_HEADS + h
        sl = slice(h * LANE, (h + 1) * LANE)
        gcol = gc_all[:, 2 * GDN_HEADS + vh:2 * GDN_HEADS + vh + 1]
        heads.append(dict(
            vh=vh, sl=sl, o_ref=o_ref, incl=incl, strict=strict,
            qh=q_ref[0, :, sl], kh=k_ref[0, :, sl], vv=v_ref[0, :, sl],
            beta=bg[:, vh:vh + 1],
            gcol=gcol,
            grow=gr_all[vh:vh + 1, :],
            g_last=gcol[last:last + 1, :]))
    return heads


def _gdn_kernel(qf_ref, kf_ref, vf_ref, bgf_ref, gtf_ref, qr_ref, kr_ref, vr_ref, bgr_ref, gtr_ref,
                of_ref, or_ref, s_ref):
    @pl.when(pl.program_id(0) == 0)
    def _():
        s_ref[...] = jnp.zeros(s_ref.shape, F32)

    c = GDN_CHUNK
    hs = (_gdn_gates(qf_ref, kf_ref, vf_ref, bgf_ref, gtf_ref, of_ref, 0)
          + _gdn_gates(qr_ref, kr_ref, vr_ref, bgr_ref, gtr_ref, or_ref, 1))
    row = lax.broadcasted_iota(jnp.int32, (c, c), 0)
    col = lax.broadcasted_iota(jnp.int32, (c, c), 1)
    eye = jnp.where(row == col, 1.0, 0.0)
    nt = (((1,), (1,)), ((), ()))
    for t in hs:
        kb = t['kh'].astype(BF16)
        t['qk_kk'] = lax.dot_general(jnp.concatenate([t['qh'].astype(BF16), kb], axis=0), kb, nt,
                                     preferred_element_type=F32)
    for t in hs:
        decay = jnp.exp(jnp.where(t['incl'], t['gcol'] - t['grow'], -jnp.inf))
        t['qk'] = t['qk_kk'][0:c] * decay
        a = jnp.where(t['strict'], t['qk_kk'][c:2 * c] * t['beta'] * decay, 0.0)
        t['pw'] = a
        t['inv'] = eye - a
    for _ in range(5):
        for t in hs:
            t['pw'] = _bdot(t['pw'], t['pw'])
        for t in hs:
            t['inv'] = t['inv'] + _bdot(t['inv'], t['pw'])
    for t in hs:
        eg = jnp.exp(t['gcol'])
        rhs = jnp.concatenate([t['vv'] * t['beta'], t['kh'] * (t['beta'] * eg)], axis=1)
        t['uw'] = _bdot(t['inv'], rhs)
        t['qd'] = t['qh'] * eg
    for t in hs:
        t['s_old'] = s_ref[t['vh']]
        t['ws_qs'] = _bdot(jnp.concatenate([t['uw'][:, LANE:2 * LANE], t['qd']], axis=0), t['s_old'])
    for t in hs:
        t['v_new'] = t['uw'][:, 0:LANE] - t['ws_qs'][0:c]
        t['o_ref'][:, t['sl']] = t['ws_qs'][c:2 * c] + _bdot(t['qk'], t['v_new'])
    for t in hs:
        kt = t['kh'] * jnp.exp(t['g_last'] - t['gcol'])
        s_ref[t['vh']] = t['s_old'] * jnp.exp(t['g_last']) + _bdot(kt.T, t['v_new'])


def gdn_call(qkv, bg, gt, n_lat):
    t = qkv.shape[1]
    c = GDN_CHUNK
    n = t // c
    n_ctx = (t - n_lat) // c

    def fwd(s):
        return (s + n - n_ctx) % n

    def rev(s):
        return n - 1 - s

    def specs(order):
        return [pl.BlockSpec((1, c, GDN_W), lambda s: (0, order(s), 0)),
                pl.BlockSpec((1, c, GDN_W), lambda s: (1, order(s), 0)),
                pl.BlockSpec((1, c, GDN_W), lambda s: (2, order(s), 0)),
                pl.BlockSpec((c, LANE), lambda s: (order(s), 0)),
                pl.BlockSpec((1, 2 * GDN_HEADS, c), lambda s: (order(s), 0, 0))]

    return pl.pallas_call(
        _gdn_kernel,
        out_shape=(jax.ShapeDtypeStruct((t, GDN_W), F32), jax.ShapeDtypeStruct((t, GDN_W), F32)),
        grid=(n,),
        in_specs=specs(fwd) + specs(rev),
        out_specs=(pl.BlockSpec((c, GDN_W), lambda s: (fwd(s), 0)), pl.BlockSpec((c, GDN_W), lambda s: (rev(s), 0))),
        scratch_shapes=[pltpu.VMEM((2 * GDN_HEADS, GDN_DK, GDN_DV), F32)],
        compiler_params=_cparams(("arbitrary",)),
        name="gdn_scan",
    )(qkv, qkv, qkv, bg, gt, qkv, qkv, qkv, bg, gt)


def _rg_direction(x_ref, p_ref, n_ref, cw_ref, cb_ref, wa_ref, ba_ref, wx_ref, bx_ref, lam_ref, o_ref,
                  h_ref, ext_ref, tile, n_lat_tiles, d):
    t = ROW_TILE
    has_prev, has_next = _halo_flags(tile, n_lat_tiles)
    x = _conv4(x_ref, p_ref, n_ref, cw_ref[...], ext_ref, has_prev, has_next) + cb_ref[...]
    xb = x.astype(BF16)
    r_parts, i_parts = [], []
    for n in range(RG_BLOCKS):
        seg = xb[:, n * RG_BS:(n + 1) * RG_BS]
        r_parts.append(jnp.dot(seg, wa_ref[0, n].astype(BF16), preferred_element_type=F32))
        i_parts.append(jnp.dot(seg, wx_ref[0, n].astype(BF16), preferred_element_type=F32))
    r = _sigmoid(jnp.concatenate(r_parts, axis=1) + ba_ref[0])
    ig = _sigmoid(jnp.concatenate(i_parts, axis=1) + bx_ref[0])
    log_a = -RG_C * r * _softplus(-lam_ref[0])
    a = jnp.exp(log_a)
    u = jnp.sqrt(_one_minus_exp(2.0 * log_a)) * (ig * x)
    sub = lax.broadcasted_iota(jnp.int32, (t, 1), 0) & (SUBLANES - 1)
    sh = 1
    while sh < SUBLANES:
        if d == 0:
            a_s = pltpu.roll(a, sh, 0)
            u_s = pltpu.roll(u, sh, 0)
            ok = sub >= sh
        else:
            a_s = pltpu.roll(a, t - sh, 0)
            u_s = pltpu.roll(u, t - sh, 0)
            ok = sub < SUBLANES - sh
        u = jnp.where(ok, a * u_s + u, u)
        a = jnp.where(ok, a * a_s, a)
        sh *= 2
    carry = h_ref[d:d + 1, :]
    n_groups = t // SUBLANES
    for g in (range(n_groups) if d == 0 else range(n_groups - 1, -1, -1)):
        rows = slice(g * SUBLANES, (g + 1) * SUBLANES)
        h_g = u[rows, :] + a[rows, :] * carry
        o_ref[rows, :] = h_g
        carry = h_g[SUBLANES - 1:SUBLANES, :] if d == 0 else h_g[0:1, :]
    h_ref[d:d + 1, :] = carry


def _rg_kernel(xf_ref, pf_ref, nf_ref, xr_ref, pr_ref, nr_ref, cw_ref, cb_ref,
               waf_ref, baf_ref, wxf_ref, bxf_ref, lamf_ref, war_ref, bar_ref, wxr_ref, bxr_ref, lamr_ref,
               of_ref, or_ref, h_ref, ext_ref, *, n_tiles, n_lat_tiles):
    s = pl.program_id(0)

    @pl.when(s == 0)
    def _():
        h_ref[...] = jnp.zeros(h_ref.shape, F32)

    tile_f = (s + n_lat_tiles) % n_tiles
    tile_r = n_tiles - 1 - s
    _rg_direction(xf_ref, pf_ref, nf_ref, cw_ref, cb_ref, waf_ref, baf_ref, wxf_ref, bxf_ref, lamf_ref, of_ref,
                  h_ref, ext_ref, tile_f, n_lat_tiles, 0)
    _rg_direction(xr_ref, pr_ref, nr_ref, cw_ref, cb_ref, war_ref, bar_ref, wxr_ref, bxr_ref, lamr_ref, or_ref,
                  h_ref, ext_ref, tile_r, n_lat_tiles, 1)


def rg_call(z, conv_w, conv_b, w_a, b_a, w_x, b_x, lam, n_lat):
    t = z.shape[0]
    n_tiles = t // ROW_TILE
    n_lat_tiles = n_lat // ROW_TILE
    r8 = ROW_TILE // 8
    last8 = n_tiles * r8 - 1
    colb = Z_RX // RG_WIDTH

    def fwd(s):
        return (s + n_lat_tiles) % n_tiles

    def rev(s):
        return n_tiles - 1 - s

    def tile_specs(order):
        return [pl.BlockSpec((ROW_TILE, RG_WIDTH), lambda s: (order(s), colb)),
                pl.BlockSpec((8, RG_WIDTH), lambda s: (jnp.maximum(order(s) * r8 - 1, 0), colb)),
                pl.BlockSpec((8, RG_WIDTH), lambda s: (jnp.minimum((order(s) + 1) * r8, last8), colb))]

    def dir_specs(d):
        return [pl.BlockSpec((1, RG_BLOCKS, RG_BS, RG_BS), lambda s: (d, 0, 0, 0)),
                pl.BlockSpec((1, 1, RG_WIDTH), lambda s: (d, 0, 0)),
                pl.BlockSpec((1, RG_BLOCKS, RG_BS, RG_BS), lambda s: (d, 0, 0, 0)),
                pl.BlockSpec((1, 1, RG_WIDTH), lambda s: (d, 0, 0)),
                pl.BlockSpec((1, 1, RG_WIDTH), lambda s: (d, 0, 0))]

    b_a3 = b_a.reshape(2, 1, RG_WIDTH)
    b_x3 = b_x.reshape(2, 1, RG_WIDTH)
    lam3 = lam.reshape(2, 1, RG_WIDTH)
    return pl.pallas_call(
        functools.partial(_rg_kernel, n_tiles=n_tiles, n_lat_tiles=n_lat_tiles),
        out_shape=(jax.ShapeDtypeStruct((t, RG_WIDTH), F32), jax.ShapeDtypeStruct((t, RG_WIDTH), F32)),
        grid=(n_tiles,),
        in_specs=tile_specs(fwd) + tile_specs(rev)
        + [pl.BlockSpec((4, RG_WIDTH), lambda s: (0, 0)), pl.BlockSpec((1, RG_WIDTH), lambda s: (0, 0))]
        + dir_specs(0) + dir_specs(1),
        out_specs=(pl.BlockSpec((ROW_TILE, RG_WIDTH), lambda s: (fwd(s), 0)),
                   pl.BlockSpec((ROW_TILE, RG_WIDTH), lambda s: (rev(s), 0))),
        scratch_shapes=[pltpu.VMEM((8, RG_WIDTH), F32), pltpu.VMEM((ROW_TILE + 16, RG_WIDTH), F32)],
        compiler_params=_cparams(("arbitrary",)),
        name="rg_lru",
    )(z, z, z, z, z, z, conv_w, conv_b, w_a, b_a3, w_x, b_x3, lam3, w_a, b_a3, w_x, b_x3, lam3)


def _assemble_kernel(of_ref, or_ref, gz_ref, on_ref, hf_ref, hr_ref, ry_ref, og_ref, org_ref):
    o = of_ref[...] + or_ref[...]
    gate = _silu(gz_ref[...])
    for h in range(GDN_HEADS):
        sl = slice(h * LANE, (h + 1) * LANE)
        seg = o[:, sl]
        on = seg * lax.rsqrt(jnp.mean(seg * seg, axis=-1, keepdims=True) + EPS) * on_ref[...]
        og_ref[:, sl] = (on * gate[:, sl]).astype(BF16)
    org_ref[...] = ((hf_ref[...] + hr_ref[...]) * _gelu_tanh(ry_ref[...])).astype(BF16)


def assemble_call(of, orv, z, o_norm, hf, hr, m):
    row = pl.BlockSpec((ROW_TILE, 1024), lambda i: (i, 0))
    return pl.pallas_call(
        _assemble_kernel,
        out_shape=(jax.ShapeDtypeStruct((m, GDN_W), BF16), jax.ShapeDtypeStruct((m, RG_WIDTH), BF16)),
        grid=(m // ROW_TILE,),
        in_specs=[row, row, pl.BlockSpec((ROW_TILE, GDN_W), lambda i: (i, Z_GZ // GDN_W)),
                  pl.BlockSpec((1, LANE), lambda i: (0, 0)), row, row,
                  pl.BlockSpec((ROW_TILE, RG_WIDTH), lambda i: (i, Z_RY // RG_WIDTH))],
        out_specs=(row, row),
        compiler_params=_cparams(("arbitrary",)),
        name="mix_assemble",
    )(of, orv, z, o_norm, hf, hr, z)


def _post_mix_kernel(y_ref, x_ref, gpost_ref, gm_ref, gpre_ref, sc_ref, sh_ref, wr_ref,
                     x1_ref, h2_ref, aff_ref, *, ctx_tile):
    is_ctx = pl.program_id(0) >= ctx_tile

    def pick(ref):
        return jnp.where(is_ctx, ref[1:2, :], ref[0:1, :])

    x1 = x_ref[...] + pick(gm_ref) * _rms(y_ref[...], gpost_ref[...])
    x1_ref[...] = x1
    h2 = _rms(x1, gpre_ref[...]) * (1.0 + pick(sc_ref)) + pick(sh_ref)
    h2b = h2.astype(BF16)
    h2r = h2b.astype(F32)
    h2_ref[...] = _pack_bf16_pair(h2r[:, 0:D_MODEL // 2], h2r[:, D_MODEL // 2:])
    h_lo = (h2 - h2b.astype(F32)).astype(BF16)
    wr = wr_ref[...]
    w_hi = wr.astype(BF16)
    w_lo = (wr - w_hi.astype(F32)).astype(BF16)
    nt = (((1,), (1,)), ((), ()))
    logits = (lax.dot_general(w_hi, h2b, nt, preferred_element_type=F32)
              + lax.dot_general(w_hi, h_lo, nt, preferred_element_type=F32)
              + lax.dot_general(w_lo, h2b, nt, preferred_element_type=F32))
    e = jnp.exp(logits - jnp.max(logits, axis=0, keepdims=True))
    aff_ref[...] = e / jnp.sum(e, axis=0, keepdims=True)


def post_mix_call(y, x, gpost, gm, gpre, sc, sh, wr_t, n_lat):
    m = y.shape[0]
    row = pl.BlockSpec((ROW_TILE, D_MODEL), lambda i: (i, 0))
    vec1 = pl.BlockSpec((1, D_MODEL), lambda i: (0, 0))
    vec2 = pl.BlockSpec((2, D_MODEL), lambda i: (0, 0))
    return pl.pallas_call(
        functools.partial(_post_mix_kernel, ctx_tile=n_lat // ROW_TILE),
        out_shape=(jax.ShapeDtypeStruct((m, D_MODEL), F32), jax.ShapeDtypeStruct((m, D_MODEL // 2), jnp.uint32),
                   jax.ShapeDtypeStruct((N_EXPERTS, m), F32)),
        grid=(m // ROW_TILE,),
        in_specs=[row, row, vec1, vec2, vec1, vec2, vec2, pl.BlockSpec((N_EXPERTS, D_MODEL), lambda i: (0, 0))],
        out_specs=(row, pl.BlockSpec((ROW_TILE, D_MODEL // 2), lambda i: (i, 0)),
                   pl.BlockSpec((N_EXPERTS, ROW_TILE), lambda i: (0, i))),
        compiler_params=_cparams(("arbitrary",)),
        name="post_mix",
    )(y, x, gpost, gm, gpre, sc, sh, wr_t)


def _post_ffn_kernel(y_ref, x_ref, g_ref, gf_ref, o_ref, *, ctx_tile):
    is_ctx = pl.program_id(0) >= ctx_tile
    gf = jnp.where(is_ctx, gf_ref[1:2, :], gf_ref[0:1, :])
    o_ref[...] = x_ref[...] + gf * _rms(y_ref[...], g_ref[...])


def post_ffn_call(y, x, gain, gf, n_lat):
    m = y.shape[0]
    row = pl.BlockSpec((ROW_TILE, D_MODEL), lambda i: (i, 0))
    return pl.pallas_call(
        functools.partial(_post_ffn_kernel, ctx_tile=n_lat // ROW_TILE),
        out_shape=jax.ShapeDtypeStruct((m, D_MODEL), F32),
        grid=(m // ROW_TILE,),
        in_specs=[row, row, pl.BlockSpec((1, D_MODEL), lambda i: (0, 0)), pl.BlockSpec((2, D_MODEL), lambda i: (0, 0))],
        out_specs=row,
        compiler_params=_cparams(("arbitrary",)),
        name="post_ffn",
    )(y, x, gain, gf)


def _pack_bf16_pair(lo, hi):
    lo_bits = lax.bitcast_convert_type(lo, jnp.uint32) >> 16
    hi_bits = lax.bitcast_convert_type(hi, jnp.uint32) & jnp.uint32(0xFFFF0000)
    return lo_bits | hi_bits


def _unpack_bf16_pair(w):
    lo = lax.bitcast_convert_type(w << 16, F32).astype(BF16)
    hi = lax.bitcast_convert_type(w & jnp.uint32(0xFFFF0000), F32).astype(BF16)
    return lo, hi


def _moe_up_kernel(idx_ref, h_hbm, wg_ref, wu_ref, o_ref, xbuf, sem, *, cap, n_ff_tiles):
    e = pl.program_id(0)
    n_e = pl.num_programs(0)
    half = D_MODEL // 2

    f = pl.program_id(1)
    n_f = pl.num_programs(1)
    slot = e % 2
    part = cap // n_ff_tiles

    def row_copy(expert, s, dst_slot):
        row = idx_ref[expert, s]
        return pltpu.make_async_copy(h_hbm.at[pl.ds(row, 1), :], xbuf.at[dst_slot, pl.ds(s, 1), :], sem.at[dst_slot])

    def wait_rows(dst_slot):
        pltpu.make_async_copy(h_hbm.at[pl.ds(0, cap), :], xbuf.at[dst_slot], sem.at[dst_slot]).wait()

    @pl.when(f == 0)
    def _():
        @pl.when(e == 0)
        def _():
            def body(s, carry):
                row_copy(0, s, 0).start()
                return carry
            lax.fori_loop(0, cap, body, 0)

        wait_rows(slot)

    nxt = jnp.minimum(e + 1, n_e - 1)
    for j in range(part):
        row_copy(nxt, f * part + j, 1 - slot).start()

    x_lo, x_hi = _unpack_bf16_pair(xbuf[slot])

    def proj(w_ref):
        return (jnp.dot(x_lo, w_ref[0, 0, 0:half, :].astype(BF16), preferred_element_type=F32)
                + jnp.dot(x_hi, w_ref[0, 0, half:, :].astype(BF16), preferred_element_type=F32))

    o_ref[0] = (_silu(proj(wg_ref)) * proj(wu_ref)).astype(BF16)

    @pl.when(jnp.logical_and(e == n_e - 1, f == n_f - 1))
    def _():
        wait_rows(1 - slot)


def moe_up_call(idx, h_packed, w_gate, w_up, l, tf=256):
    e, cap = idx.shape
    d = D_MODEL
    return pl.pallas_call(
        functools.partial(_moe_up_kernel, cap=cap, n_ff_tiles=EXPERT_FF // tf),
        out_shape=jax.ShapeDtypeStruct((e, cap, EXPERT_FF), BF16),
        grid_spec=pltpu.PrefetchScalarGridSpec(
            num_scalar_prefetch=1,
            grid=(e, EXPERT_FF // tf),
            in_specs=[pl.BlockSpec(memory_space=pl.ANY),
                      pl.BlockSpec((1, 1, d, tf), lambda i, f, idx_ref: (l, i, 0, f)),
                      pl.BlockSpec((1, 1, d, tf), lambda i, f, idx_ref: (l, i, 0, f))],
            out_specs=pl.BlockSpec((1, cap, tf), lambda i, f, idx_ref: (i, 0, f)),
            scratch_shapes=[pltpu.VMEM((2, cap, d // 2), jnp.uint32), pltpu.SemaphoreType.DMA((2,))]),
        compiler_params=_cparams(("arbitrary", "arbitrary")),
        name="moe_up",
    )(idx, h_packed, w_gate, w_up)


def _moe_scatter_kernel(idx_ref, y_ref, acc_in, acc_hbm, buf, sem_in, sem_out, *, rows):
    del acc_in
    e = pl.program_id(0)
    base = pl.program_id(1) * rows

    def gather(s, carry):
        row = idx_ref[e, base + s]
        pltpu.make_async_copy(acc_hbm.at[pl.ds(row, 1), :], buf.at[pl.ds(s, 1), :], sem_in).start()
        return carry

    def scatter(s, carry):
        row = idx_ref[e, base + s]
        pltpu.make_async_copy(buf.at[pl.ds(s, 1), :], acc_hbm.at[pl.ds(row, 1), :], sem_out).start()
        return carry

    lax.fori_loop(0, rows, gather, 0, unroll=8)
    pltpu.make_async_copy(acc_hbm.at[pl.ds(0, rows), :], buf, sem_in).wait()
    buf[...] = buf[...] + y_ref[0]
    lax.fori_loop(0, rows, scatter, 0, unroll=8)
    pltpu.make_async_copy(buf, acc_hbm.at[pl.ds(0, rows), :], sem_out).wait()


def moe_scatter_call(idx, y_e, m):
    e, cap, d = y_e.shape
    rows = cap // 2
    return pl.pallas_call(
        functools.partial(_moe_scatter_kernel, rows=rows),
        out_shape=jax.ShapeDtypeStruct((m, d), F32),
        grid_spec=pltpu.PrefetchScalarGridSpec(
            num_scalar_prefetch=1,
            grid=(e, 2),
            in_specs=[pl.BlockSpec((1, rows, d), lambda i, hf, idx_ref: (i, hf, 0)),
                      pl.BlockSpec(memory_space=pl.ANY)],
            out_specs=pl.BlockSpec(memory_space=pl.ANY),
            scratch_shapes=[pltpu.VMEM((rows, d), F32), pltpu.SemaphoreType.DMA(()), pltpu.SemaphoreType.DMA(())]),
        input_output_aliases={2: 0},
        compiler_params=_cparams(("arbitrary", "arbitrary")),
        name="moe_scatter",
    )(idx, y_e, jnp.zeros((m, d), F32))


def _moe_down_kernel(h_ref, wd_ref, g_ref, o_ref):
    y = jnp.dot(h_ref[0], wd_ref[0, 0].astype(BF16), preferred_element_type=F32)
    o_ref[0] = y * g_ref[0]


def moe_down_call(hid, w_down, l, gates, tn=1024):
    e, c, f = hid.shape
    d = w_down.shape[3]
    return pl.pallas_call(
        _moe_down_kernel,
        out_shape=jax.ShapeDtypeStruct((e, c, d), F32),
        grid=(e, d // tn),
        in_specs=[pl.BlockSpec((1, c, f), lambda i, n: (i, 0, 0)),
                  pl.BlockSpec((1, 1, f, tn), lambda i, n: (l, i, 0, n)),
                  pl.BlockSpec((1, c, 1), lambda i, n: (i, 0, 0))],
        out_specs=pl.BlockSpec((1, c, tn), lambda i, n: (i, 0, n)),
        compiler_params=_cparams(("arbitrary", "arbitrary")),
        name="moe_down",
    )(hid, w_down, gates)


def _rope_tables(n_lat, n_ctx):
    rows = n_lat // GRID_W
    row = jnp.broadcast_to(jnp.arange(rows, dtype=F32)[:, None], (rows, GRID_W)).reshape(-1)
    col = jnp.broadcast_to(jnp.arange(GRID_W, dtype=F32)[None, :], (rows, GRID_W)).reshape(-1)
    n_freq = MLA_ROPE // 4
    inv_freq = ROPE_THETA ** (-jnp.arange(n_freq, dtype=F32) / n_freq)
    ang = jnp.concatenate([row[:, None] * inv_freq, col[:, None] * inv_freq], axis=-1)
    cos = jnp.cos(ang)
    sin = jnp.sin(ang)
    cos64 = jnp.concatenate([cos, cos], axis=-1)
    sin64 = jnp.concatenate([sin, sin], axis=-1)
    one_l = jnp.ones((n_lat, 64), F32)
    zero_l = jnp.zeros((n_lat, 64), F32)
    one_c = jnp.ones((n_ctx, 64), F32)
    zero_c = jnp.zeros((n_ctx, 64), F32)
    cq = jnp.concatenate([jnp.concatenate([cos64, one_l], 1), jnp.concatenate([one_c, one_c], 1)], 0)
    sq = jnp.concatenate([jnp.concatenate([sin64, zero_l], 1), jnp.concatenate([zero_c, zero_c], 1)], 0)
    ck = jnp.concatenate([jnp.concatenate([cos64, zero_l], 1), jnp.concatenate([zero_c, one_c], 1)], 0)
    sk = jnp.concatenate([jnp.concatenate([sin64, zero_l], 1), jnp.concatenate([zero_c, zero_c], 1)], 0)
    return cq, sq, ck, sk


def _rot_half_cols(w):
    half = MLA_ROPE // 2
    return jnp.concatenate([-w[..., half:], w[..., :half]], axis=-1)


def _relayout_w_in(w_in):
    q_a = w_in[:, 0:1024]
    kv_a = w_in[:, 1024:1536]
    k_pe = w_in[:, 1536:1600]
    g_qkv = w_in[:, 1600:4672]
    g_z = w_in[:, 4672:5696]
    g_ba = w_in[:, 5696:5728]
    r_x = w_in[:, 5728:6752]
    r_y = w_in[:, 6752:7776]
    zeros = functools.partial(jnp.zeros, dtype=w_in.dtype)
    d = w_in.shape[0]
    return jnp.concatenate([q_a, g_qkv, g_z, r_x, r_y, kv_a, k_pe, k_pe, _rot_half_cols(k_pe), zeros((d, 64)),
                            g_ba, zeros((d, LANE - 32)), zeros((d, LANE))], axis=1)


def _relayout_w_qb(w_qb):
    w = w_qb.reshape(MLA_Q_RANK, MLA_HEADS, MLA_NOPE + MLA_ROPE)
    nope = w[:, :, :MLA_NOPE]
    pe = w[:, :, MLA_NOPE:]
    zeros = jnp.zeros_like(pe)
    out = jnp.concatenate([nope, pe, pe, _rot_half_cols(pe), zeros], axis=-1)
    return jnp.transpose(out, (1, 0, 2))


BIG_WEIGHTS = ('mod_w', 'mla_w_kvb', 'w_out', 'w_gate', 'w_up', 'w_down')


def _layer(x_all, c_rows, p, big, l, n_lat, n_ctx, update_ctx):
    t = n_lat + n_ctx
    mod = mod_call(c_rows, big['mod_w'], p['mod_b'][None, :], l)[0:2]
    sh_m, sc_m, g_m, sh_f, sc_f, g_f = [mod[:, i * D_MODEL:(i + 1) * D_MODEL] for i in range(6)]

    h = prenorm_call(x_all, p['norm_mix_pre'][None, :], sc_m, sh_m, n_lat)
    z = matmul_call(h, _relayout_w_in(p['w_in']), name="w_in")

    cq, sq, ck, sk = _rope_tables(n_lat, n_ctx)
    q = qproj_call(z, p['mla_q_norm'][None, :], _relayout_w_qb(p['mla_w_qb']), cq, sq)
    tk = next(c for c in (768, 256) if t % c == 0)
    k, v = kvproj_call(z, p['mla_kv_norm'][None, :], big['mla_w_kvb'], l, ck, sk, tk)
    o_mla = attn_call(q, k, v, n_lat, 0, t, 0, tq=ATTN_TQ, tk=tk)
    if update_ctx:
        o_mla_c = attn_call(q, k, v, n_ctx, n_lat, n_ctx, n_lat, tq=n_ctx, tk=n_ctx)
        o_mla = jnp.concatenate([o_mla, o_mla_c], axis=0)

    pad = jnp.zeros((2 * GDN_HEADS,), F32)
    alog_row = jnp.concatenate([pad, p['gdn_a_log'].reshape(-1), jnp.zeros((LANE - 4 * GDN_HEADS,), F32)])[None, :]
    dtb_row = jnp.concatenate([pad, p['gdn_dt_bias'].reshape(-1), jnp.zeros((LANE - 4 * GDN_HEADS,), F32)])[None, :]
    qkv, bg = gdn_prep_call(z, p['gdn_conv_w'], alog_row, dtb_row, n_lat)
    gt = jnp.swapaxes(bg[:, 2 * GDN_HEADS:4 * GDN_HEADS].reshape(t // GDN_CHUNK, GDN_CHUNK, 2 * GDN_HEADS), 1, 2)
    o_f, o_r = gdn_call(qkv, bg, gt, n_lat)

    h_f, h_r = rg_call(z, p['rg_conv_w'], p['rg_conv_b'][None, :], p['rg_w_a'], p['rg_b_a'], p['rg_w_x'],
                       p['rg_b_x'], p['rg_lambda'], n_lat)

    m = t if update_ctx else n_lat
    o_gdn, o_rg = assemble_call(o_f, o_r, z, p['gdn_o_norm'][None, :], h_f, h_r, m)
    y = matmul3_call(o_mla, o_gdn, o_rg, big['w_out'], l, m)
    x1, h2, aff_t = post_mix_call(y, x_all, p['norm_mix_post'][None, :], g_m, p['norm_ffn_pre'][None, :],
                                  sc_f, sh_f, p['w_router'].T, n_lat)

    cap = CAPACITY_FACTOR * n_lat // N_EXPERTS
    gates, idx = lax.top_k(aff_t[:, :n_lat], cap)
    if update_ctx:
        cap_c = CAPACITY_FACTOR * n_ctx // N_EXPERTS
        gates_c, idx_c = lax.top_k(aff_t[:, n_lat:], cap_c)
        gates = jnp.concatenate([gates, gates_c], axis=1)
        idx = jnp.concatenate([idx, idx_c + n_lat], axis=1)
    hid = moe_up_call(idx, h2, big['w_gate'], big['w_up'], l)
    y_e = moe_down_call(hid, big['w_down'], l, gates[..., None])
    y_moe = moe_scatter_call(idx, y_e, m)
    return post_ffn_call(y_moe, x1, p['norm_ffn_post'][None, :], g_f, n_lat)


def kernel(x, c, ctx, c_ctx, mod_w, mod_b, norm_mix_pre, norm_mix_post, norm_ffn_pre, norm_ffn_post,
           w_in, mla_q_norm, mla_kv_norm, mla_w_qb, mla_w_kvb, gdn_conv_w, gdn_a_log, gdn_dt_bias,
           gdn_o_norm, rg_conv_w, rg_conv_b, rg_w_a, rg_b_a, rg_w_x, rg_b_x, rg_lambda, w_out,
           w_router, w_gate, w_up, w_down):
    assert x.shape[0] == 1 and ctx.shape[0] == 1
    n_lat, n_ctx = x.shape[1], ctx.shape[1]
    assert n_ctx == ROW_TILE and n_lat % ATTN_TQ == 0
    depth = mod_w.shape[0]
    stacked = dict(mod_w=mod_w, mod_b=mod_b, norm_mix_pre=norm_mix_pre, norm_mix_post=norm_mix_post,
                   norm_ffn_pre=norm_ffn_pre, norm_ffn_post=norm_ffn_post, w_in=w_in, mla_q_norm=mla_q_norm,
                   mla_kv_norm=mla_kv_norm, mla_w_qb=mla_w_qb, mla_w_kvb=mla_w_kvb, gdn_conv_w=gdn_conv_w,
                   gdn_a_log=gdn_a_log, gdn_dt_bias=gdn_dt_bias, gdn_o_norm=gdn_o_norm, rg_conv_w=rg_conv_w,
                   rg_conv_b=rg_conv_b, rg_w_a=rg_w_a, rg_b_a=rg_b_a, rg_w_x=rg_w_x, rg_b_x=rg_b_x,
                   rg_lambda=rg_lambda, w_out=w_out, w_router=w_router, w_gate=w_gate, w_up=w_up, w_down=w_down)
    c_rows = jnp.concatenate([c, c_ctx[None, :], jnp.zeros((6, D_MODEL), F32)], axis=0)
    x_all = jnp.concatenate([x[0], ctx[0]], axis=0)
    big = {k: stacked[k] for k in BIG_WEIGHTS}
    for l in range(depth):
        p = {k: v[l] for k, v in stacked.items() if k not in BIG_WEIGHTS}
        x_all = _layer(x_all, c_rows, p, big, l, n_lat, n_ctx, l < depth - 1)
    return x_all[None]
```
